```python
import math, functools
import jax, jax.numpy as jnp
from jax import lax
import numpy as np

D_MODEL = 2048
BATCH = 1
SEQ = 8192
DEPTH = 1
DEC_BATCH = 32
DEC_SEQ = 4
PAST_LEN = 16384
PAGE_SIZE = 128

DA_HEADS = 8
DA_KV_HEADS = 4
DA_HEAD_DIM = 128
DA_REP = DA_HEADS // DA_KV_HEADS
Q_BLOCK = 128
GD_K_HEADS = 16
GD_V_HEADS = 32
GD_K_DIM = 128
GD_V_DIM = 128
GD_REP = GD_V_HEADS // GD_K_HEADS
CONV_W = 4
CONV_DIM = 2 * GD_K_HEADS * GD_K_DIM + GD_V_HEADS * GD_V_DIM
CHUNK = 64
D_FF = 4 * D_MODEL
NORM_EPS = 1e-5
L2_EPS = 1e-6
NEG_INF = -1e30
DEEPNORM_ALPHA = (2.0 * DEPTH) ** 0.25
DEEPNORM_BETA = (8.0 * DEPTH) ** -0.25

IN_SPLITS = (
    DA_HEADS * 2 * DA_HEAD_DIM,
    DA_KV_HEADS * 2 * DA_HEAD_DIM,
    DA_KV_HEADS * 2 * DA_HEAD_DIM,
    GD_K_HEADS * GD_K_DIM,
    GD_K_HEADS * GD_K_DIM,
    GD_V_HEADS * GD_V_DIM,
    GD_V_HEADS * GD_V_DIM,
    GD_V_HEADS,
    GD_V_HEADS,
    2 * D_MODEL,
)
IN_DIM = sum(IN_SPLITS)

kernel_name = "diffattn_gdn_parallel_deepnorm_step"


def lambda_init_fn(layer):
    return 0.8 - 0.6 * math.exp(-0.3 * layer)


def rms_norm(x, w):
    xf = x.astype(jnp.float32)
    y = xf * lax.rsqrt(jnp.mean(xf * xf, axis=-1, keepdims=True) + NORM_EPS)
    return y * w.astype(jnp.float32)


def layer_norm(x, g, b):
    xf = x.astype(jnp.float32)
    mu = jnp.mean(xf, axis=-1, keepdims=True)
    xc = xf - mu
    var = jnp.mean(xc * xc, axis=-1, keepdims=True)
    y = xc * lax.rsqrt(var + NORM_EPS) * g.astype(jnp.float32) + b.astype(jnp.float32)
    return y.astype(x.dtype)


def l2_normalize(x):
    return x * lax.rsqrt(jnp.sum(x * x, axis=-1, keepdims=True) + L2_EPS)


def in_projection(x, w_in):
    h = jnp.einsum('bld,de->ble', x, w_in)
    bounds = [0]
    for n in IN_SPLITS:
        bounds.append(bounds[-1] + n)
    return [h[..., s:e] for s, e in zip(bounds[:-1], bounds[1:])]


def gather_pages(pool, page_table):
    g = pool[page_table]
    return g.reshape((g.shape[0], g.shape[1] * g.shape[2]) + g.shape[3:])


def diff_attn_prompt(q, k, v, lam):
    B, S = q.shape[:2]
    qblk = min(Q_BLOCK, S)
    nb = S // qblk
    scale = DA_HEAD_DIM ** -0.5
    qb = jnp.moveaxis(q.reshape((B, nb, qblk) + q.shape[2:]), 1, 0)
    kpos = jnp.arange(S)

    def block(args):
        qi, start = args
        s = jnp.einsum('bqgrcd,bkgcd->bgrcqk', qi, k).astype(jnp.float32) * scale
        qpos = start + jnp.arange(qblk)
        s = jnp.where(kpos[None, :] <= qpos[:, None], s, NEG_INF)
        p = jax.nn.softmax(s, axis=-1)
        w = (p[:, :, :, 0] - lam * p[:, :, :, 1]).astype(v.dtype)
        return jnp.einsum('bgrqk,bkge->bqgre', w, v)

    o = lax.map(block, (qb, jnp.arange(nb) * qblk))
    return jnp.moveaxis(o, 0, 1).reshape(B, S, DA_HEADS, 2 * DA_HEAD_DIM)


def diff_attn_sample(q, k, v, lam, k_past, v_past):
    B, T = q.shape[:2]
    P = k_past.shape[1]
    scale = DA_HEAD_DIM ** -0.5
    s_past = jnp.einsum('bqgrcd,bkgcd->bgrcqk', q, k_past.astype(q.dtype)).astype(jnp.float32) * scale
    s_new = jnp.einsum('bqgrcd,bkgcd->bgrcqk', q, k).astype(jnp.float32) * scale
    causal = jnp.tril(jnp.ones((T, T), dtype=bool))
    s_new = jnp.where(causal, s_new, NEG_INF)
    p = jax.nn.softmax(jnp.concatenate([s_past, s_new], axis=-1), axis=-1)
    w = (p[:, :, :, 0] - lam * p[:, :, :, 1]).astype(v.dtype)
    o = (jnp.einsum('bgrqk,bkge->bqgre', w[..., :P], v_past.astype(v.dtype))
         + jnp.einsum('bgrqk,bkge->bqgre', w[..., P:], v))
    return o.reshape(B, T, DA_HEADS, 2 * DA_HEAD_DIM)


def short_conv(xc, conv_buf, conv_w):
    xp = jnp.concatenate([conv_buf.astype(xc.dtype), xc], axis=1)
    y = lax.conv_general_dilated(
        xp, conv_w.astype(xc.dtype)[:, None, :], window_strides=(1,), padding='VALID',
        dimension_numbers=('NWC', 'WIO', 'NWC'), feature_group_count=xc.shape[-1])
    return jax.nn.silu(y), xp[:, -(CONV_W - 1):]


def gated_delta_chunked(q, k, v, g, beta, state):
    B, L, H, DK = q.shape
    DV = v.shape[-1]
    C = min(CHUNK, L)
    pad = (-L) % C
    N = (L + pad) // C

    def to_chunks(t):
        t = t.astype(jnp.float32)
        t = jnp.pad(t, [(0, 0), (0, pad)] + [(0, 0)] * (t.ndim - 2))
        return jnp.moveaxis(t.reshape((B, N, C) + t.shape[2:]), 3, 2)

    q, k, v, g, beta = (to_chunks(t) for t in (q, k, v, g, beta))
    g = jnp.cumsum(g, axis=-1)
    tril = jnp.tril(jnp.ones((C, C), dtype=bool))
    strict = jnp.tril(jnp.ones((C, C), dtype=bool), -1)
    diff = g[..., :, None] - g[..., None, :]
    decay = jnp.where(tril, jnp.exp(jnp.where(tril, diff, 0.0)), 0.0)
    kb = k * beta[..., None]
    vb = v * beta[..., None]
    lmat = jnp.where(strict, jnp.einsum('bnhid,bnhjd->bnhij', kb, k) * decay, 0.0)
    a_mat = lmat + jnp.eye(C, dtype=jnp.float32)
    rhs = jnp.concatenate([vb, kb * jnp.exp(g)[..., None]], axis=-1)
    sol = lax.linalg.triangular_solve(a_mat, rhs, left_side=True, lower=True, unit_diagonal=True)
    u, w = sol[..., :DV], sol[..., DV:]
    qk = jnp.where(tril, jnp.einsum('bnhid,bnhjd->bnhij', q, k) * decay, 0.0)

    def step(S, xs):
        qc, kc, uc, wc, gc, qkc = xs
        v_new = uc - jnp.einsum('bhcd,bhde->bhce', wc, S)
        o = (jnp.einsum('bhcd,bhde->bhce', qc * jnp.exp(gc)[..., None], S)
             + jnp.einsum('bhij,bhje->bhie', qkc, v_new))
        g_last = gc[..., -1]
        S = (S * jnp.exp(g_last)[..., None, None]
             + jnp.einsum('bhcd,bhce->bhde', kc * jnp.exp(g_last[..., None] - gc)[..., None], v_new))
        return S, o

    xs = tuple(jnp.moveaxis(t, 1, 0) for t in (q, k, u, w, g, qk))
    S, o = lax.scan(step, state.astype(jnp.float32), xs)
    o = jnp.transpose(o, (1, 0, 3, 2, 4)).reshape(B, N * C, H, DV)[:, :L]
    return o, S


def gated_delta_branch(qd, kd, vd, z, a, b, conv_buf, ssm_state, conv_w, a_log, dt_bias, gnorm_w):
    B, L = qd.shape[:2]
    qkv, new_buf = short_conv(jnp.concatenate([qd, kd, vd], axis=-1), conv_buf, conv_w)
    nq = GD_K_HEADS * GD_K_DIM
    q = qkv[..., :nq].reshape(B, L, GD_K_HEADS, GD_K_DIM).astype(jnp.float32)
    k = qkv[..., nq:2 * nq].reshape(B, L, GD_K_HEADS, GD_K_DIM).astype(jnp.float32)
    v = qkv[..., 2 * nq:].reshape(B, L, GD_V_HEADS, GD_V_DIM)
    q = jnp.repeat(l2_normalize(q) * (GD_K_DIM ** -0.5), GD_REP, axis=2)
    k = jnp.repeat(l2_normalize(k), GD_REP, axis=2)
    beta = jax.nn.sigmoid(b.astype(jnp.float32))
    g = -jnp.exp(a_log.astype(jnp.float32)) * jax.nn.softplus(a.astype(jnp.float32) + dt_bias.astype(jnp.float32))
    o, new_state = gated_delta_chunked(q, k, v, g, beta, ssm_state)
    zf = z.reshape(B, L, GD_V_HEADS, GD_V_DIM).astype(jnp.float32)
    o = rms_norm(o, gnorm_w) * jax.nn.silu(zf)
    return o.reshape(B, L, GD_V_HEADS * GD_V_DIM).astype(qd.dtype), new_state, new_buf


def layer_forward(x, attn_fn, conv_buf, ssm_state, layer_idx, w_in, lambda_q1, lambda_k1,
                  lambda_q2, lambda_k2, subln_w, conv_w, a_log, dt_bias, gnorm_w, w_proj_a,
                  w_proj_b, w_out, ln1_g, ln1_b, w_up, w_down, ln2_g, ln2_b):
    B, L, _ = x.shape
    da_q, da_k, da_v, gd_q, gd_k, gd_v, gd_z, gd_a, gd_b, gates = in_projection(x, w_in)
    q = da_q.reshape(B, L, DA_KV_HEADS, DA_REP, 2, DA_HEAD_DIM)
    k = da_k.reshape(B, L, DA_KV_HEADS, 2, DA_HEAD_DIM)
    v = da_v.reshape(B, L, DA_KV_HEADS, 2 * DA_HEAD_DIM)
    lam_init = lambda_init_fn(layer_idx)
    lam = (jnp.exp(jnp.sum(lambda_q1.astype(jnp.float32) * lambda_k1.astype(jnp.float32)))
           - jnp.exp(jnp.sum(lambda_q2.astype(jnp.float32) * lambda_k2.astype(jnp.float32)))
           + lam_init)
    oa = attn_fn(q, k, v, lam)
    oa = (rms_norm(oa, subln_w) * (1.0 - lam_init)).reshape(B, L, -1).astype(x.dtype)
    ob, ssm_new, conv_new = gated_delta_branch(gd_q, gd_k, gd_v, gd_z, gd_a, gd_b, conv_buf,
                                               ssm_state, conv_w, a_log, dt_bias, gnorm_w)
    ya = jnp.einsum('ble,ed->bld', oa, w_proj_a)
    yb = jnp.einsum('ble,ed->bld', ob, w_proj_b)
    ga, gb = jnp.split(jax.nn.sigmoid(gates), 2, axis=-1)
    mix = jnp.einsum('bld,de->ble', ga * ya + gb * yb, w_out)
    h = layer_norm(DEEPNORM_ALPHA * x + mix, ln1_g, ln1_b)
    u = jnp.square(jax.nn.relu(jnp.einsum('bld,df->blf', h, w_up)))
    f = jnp.einsum('blf,fd->bld', u, w_down)
    y = layer_norm(DEEPNORM_ALPHA * h + f, ln2_g, ln2_b)
    return y, k, v, ssm_new, conv_new


def setup_inputs(seed: int = 0) -> dict:
    key = jax.random.key(seed)
    ks = jax.random.split(key, 32)
    n_pages = PAST_LEN // PAGE_SIZE
    n_used = DEC_BATCH * n_pages
    n_pool = n_used + max(1, n_used // 4)
    perm = jax.random.permutation(ks[0], n_pool)
    page_table = perm[:n_used].reshape(DEC_BATCH, n_pages).astype(jnp.int32)

    def nrm(k, shape, scale):
        return jax.random.normal(k, shape, jnp.float32) * scale

    dt = jnp.exp(jax.random.uniform(ks[11], (DEPTH, GD_V_HEADS), jnp.float32,
                                    minval=math.log(1e-3), maxval=math.log(1e-1)))
    return {
        "x_prompt": nrm(ks[1], (BATCH, SEQ, D_MODEL), 1.0),
        "x_sample": nrm(ks[2], (DEC_BATCH, DEC_SEQ, D_MODEL), 1.0),
        "cache_k": nrm(ks[3], (DEPTH, n_pool, PAGE_SIZE, DA_KV_HEADS, 2, DA_HEAD_DIM), 1.0),
        "cache_v": nrm(ks[4], (DEPTH, n_pool, PAGE_SIZE, DA_KV_HEADS, 2 * DA_HEAD_DIM), 1.0),
        "page_table": page_table,
        "state_ssm": nrm(ks[5], (DEPTH, DEC_BATCH, GD_V_HEADS, GD_K_DIM, GD_V_DIM), GD_K_DIM ** -0.5),
        "state_conv": nrm(ks[6], (DEPTH, DEC_BATCH, CONV_W - 1, CONV_DIM), 1.0),
        "w_in": nrm(ks[7], (DEPTH, D_MODEL, IN_DIM), D_MODEL ** -0.5),
        "lambda_q1": nrm(ks[8], (DEPTH, DA_HEAD_DIM), 0.1),
        "lambda_k1": nrm(ks[9], (DEPTH, DA_HEAD_DIM), 0.1),
        "lambda_q2": nrm(ks[10], (DEPTH, DA_HEAD_DIM), 0.1),
        "lambda_k2": nrm(ks[12], (DEPTH, DA_HEAD_DIM), 0.1),
        "subln_w": 1.0 + nrm(ks[13], (DEPTH, 2 * DA_HEAD_DIM), 0.02),
        "conv_w": nrm(ks[14], (DEPTH, CONV_W, CONV_DIM), CONV_W ** -0.5),
        "a_log": jnp.log(jax.random.uniform(ks[15], (DEPTH, GD_V_HEADS), jnp.float32, minval=1.0, maxval=16.0)),
        "dt_bias": dt + jnp.log(-jnp.expm1(-dt)),
        "gnorm_w": 1.0 + nrm(ks[16], (DEPTH, GD_V_DIM), 0.02),
        "w_proj_a": nrm(ks[17], (DEPTH, DA_HEADS * 2 * DA_HEAD_DIM, D_MODEL),
                         (DA_HEADS * 2 * DA_HEAD_DIM) ** -0.5 * DEEPNORM_BETA),
        "w_proj_b": nrm(ks[18], (DEPTH, GD_V_HEADS * GD_V_DIM, D_MODEL),
                         (GD_V_HEADS * GD_V_DIM) ** -0.5 * DEEPNORM_BETA),
        "w_out": nrm(ks[19], (DEPTH, D_MODEL, D_MODEL), D_MODEL ** -0.5 * DEEPNORM_BETA),
        "ln1_g": 1.0 + nrm(ks[20], (DEPTH, D_MODEL), 0.02),
        "ln1_b": nrm(ks[21], (DEPTH, D_MODEL), 0.02),
        "w_up": nrm(ks[22], (DEPTH, D_MODEL, D_FF), D_MODEL ** -0.5),
        "w_down": nrm(ks[23], (DEPTH, D_FF, D_MODEL), D_FF ** -0.5 * DEEPNORM_BETA),
        "ln2_g": 1.0 + nrm(ks[24], (DEPTH, D_MODEL), 0.02),
        "ln2_b": nrm(ks[25], (DEPTH, D_MODEL), 0.02),
    }


def reference(x_prompt, x_sample, cache_k, cache_v, page_table, state_ssm, state_conv, w_in,
              lambda_q1, lambda_k1, lambda_q2, lambda_k2, subln_w, conv_w, a_log, dt_bias,
              gnorm_w, w_proj_a, w_proj_b, w_out, ln1_g, ln1_b, w_up, w_down, ln2_g, ln2_b):
    yp, ys = x_prompt, x_sample
    kp_l, vp_l, sp_l, cp_l, ks_l, vs_l, ss_l, cs_l = [], [], [], [], [], [], [], []
    for l in range(DEPTH):
        params = (w_in[l], lambda_q1[l], lambda_k1[l], lambda_q2[l], lambda_k2[l], subln_w[l],
                  conv_w[l], a_log[l], dt_bias[l], gnorm_w[l], w_proj_a[l], w_proj_b[l], w_out[l],
                  ln1_g[l], ln1_b[l], w_up[l], w_down[l], ln2_g[l], ln2_b[l])
        b_p = yp.shape[0]
        zero_buf = jnp.zeros((b_p, CONV_W - 1, CONV_DIM), yp.dtype)
        zero_state = jnp.zeros((b_p, GD_V_HEADS, GD_K_DIM, GD_V_DIM), jnp.float32)
        yp, kp, vp, sp, cp = layer_forward(yp, diff_attn_prompt, zero_buf, zero_state, l, *params)
        k_past = gather_pages(cache_k[l], page_table)
        v_past = gather_pages(cache_v[l], page_table)
        attn_s = functools.partial(diff_attn_sample, k_past=k_past, v_past=v_past)
        ys, k_s, v_s, s_s, c_s = layer_forward(ys, attn_s, state_conv[l], state_ssm[l], l, *params)
        kp_l.append(kp); vp_l.append(vp); sp_l.append(sp.astype(yp.dtype)); cp_l.append(cp)
        ks_l.append(k_s); vs_l.append(v_s); ss_l.append(s_s.astype(state_ssm.dtype)); cs_l.append(c_s.astype(state_conv.dtype))
    return (yp, ys, jnp.stack(kp_l), jnp.stack(vp_l), jnp.stack(sp_l), jnp.stack(cp_l),
            jnp.stack(ks_l), jnp.stack(vs_l), jnp.stack(ss_l), jnp.stack(cs_l))
```

```python
import functools
import math

import jax
import jax.numpy as jnp
from jax import lax
from jax.experimental import pallas as pl
from jax.experimental.pallas import tpu as pltpu

F32 = jnp.float32
BF16 = jnp.bfloat16

DA_HEADS = 8
DA_KV_HEADS = 4
DA_HEAD_DIM = 128
DA_REP = DA_HEADS // DA_KV_HEADS
GD_K_HEADS = 16
GD_V_HEADS = 32
GD_DIM = 128
GD_REP = GD_V_HEADS // GD_K_HEADS
CONV_W = 4
CHUNK = 64
NORM_EPS = 1e-5
L2_EPS = 1e-6
NEG_INF = -1e30

V7X_VMEM_BYTES = 64 * 1024 * 1024
VMEM_LIMIT = V7X_VMEM_BYTES - 12 * 1024 * 1024

_NT = (((1,), (1,)), ((), ()))
_NN = (((1,), (0,)), ((), ()))


def _params(*sem):
    return pltpu.CompilerParams(dimension_semantics=sem, vmem_limit_bytes=VMEM_LIMIT)


def _dot(a, b, dn=_NN):
    return lax.dot_general(a, b, dn, preferred_element_type=F32)


def _split(a):
    hi = a.astype(BF16)
    lo = (a - hi.astype(F32)).astype(BF16)
    return hi, lo


def _dot3(a, b, dn=_NN):
    ah, al = _split(a)
    bh, bl = _split(b)
    return _dot(ah, bh, dn) + (_dot(al, bh, dn) + _dot(ah, bl, dn))


def _softplus(x):
    return jnp.maximum(x, 0.0) + jnp.log(1.0 + jnp.exp(-jnp.abs(x)))


def _sigmoid(x):
    return 1.0 / (1.0 + jnp.exp(-x))


def _lambda(lq1, lk1, lq2, lk2, lam_init):
    a = jnp.sum(lq1[...] * lk1[...], axis=-1, keepdims=True)
    b = jnp.sum(lq2[...] * lk2[...], axis=-1, keepdims=True)
    return jnp.exp(a) - jnp.exp(b) + lam_init


def _layer_norm(x, g, b):
    mu = jnp.mean(x, axis=-1, keepdims=True)
    xc = x - mu
    var = jnp.mean(xc * xc, axis=-1, keepdims=True)
    return xc * lax.rsqrt(var + NORM_EPS) * g + b


def _mm_kernel(x_ref, w_ref, *o_refs, scale):
    acc = _dot(x_ref[...], w_ref[...])
    if scale != 1.0:
        acc = acc * scale
    for o in o_refs:
        o[...] = acc.astype(o.dtype)


def _matmul(x, w, out_dtypes, *, scale=1.0, tm=512, tn=512):
    M, K = x.shape
    N = w.shape[1]
    tm = min(tm, M)
    tn = min(tn, N)
    outs = pl.pallas_call(
        functools.partial(_mm_kernel, scale=scale),
        grid=(N // tn, M // tm),
        in_specs=[pl.BlockSpec((tm, K), lambda j, i: (i, 0)),
                  pl.BlockSpec((K, tn), lambda j, i: (0, j))],
        out_specs=[pl.BlockSpec((tm, tn), lambda j, i: (i, j)) for _ in out_dtypes],
        out_shape=[jax.ShapeDtypeStruct((M, N), d) for d in out_dtypes],
        compiler_params=_params("parallel", "parallel"),
        name="in_proj",
    )(x, w)
    return outs


def _subln_out(acc0, l0, acc1, l1, lam, subw, lam_init):
    o = acc0 / l0 - lam * (acc1 / l1)
    o = o * lax.rsqrt(jnp.mean(o * o, axis=-1, keepdims=True) + NORM_EPS)
    return o * subw * (1.0 - lam_init)


def _attn_prompt_kernel(q_ref, k_ref, v_ref, subw_ref, lq1, lk1, lq2, lk2, o_ref,
                        m_sc, l_sc, acc_sc, *, t, lam_init):
    qi = pl.program_id(1)
    ki = pl.program_id(2)

    @pl.when(ki == 0)
    def _():
        m_sc[...] = jnp.full(m_sc.shape, -jnp.inf, F32)
        l_sc[...] = jnp.zeros(l_sc.shape, F32)
        acc_sc[...] = jnp.zeros(acc_sc.shape, F32)

    def step(masked):
        v = v_ref[...]
        if masked:
            row = lax.broadcasted_iota(jnp.int32, (t, t), 0)
            col = lax.broadcasted_iota(jnp.int32, (t, t), 1)
            keep = col <= row
        for r in range(DA_REP):
            for c in range(2):
                idx = r * 2 + c
                q = q_ref[:, idx * DA_HEAD_DIM:(idx + 1) * DA_HEAD_DIM]
                k = k_ref[:, c * DA_HEAD_DIM:(c + 1) * DA_HEAD_DIM]
                s = _dot(q, k, _NT)
                if masked:
                    s = jnp.where(keep, s, NEG_INF)
                m_prev = m_sc[idx]
                m_new = jnp.maximum(m_prev, jnp.max(s, axis=-1, keepdims=True))
                alpha = jnp.exp(m_prev - m_new)
                p = jnp.exp(s - m_new)
                l_sc[idx] = alpha * l_sc[idx] + jnp.sum(p, axis=-1, keepdims=True)
                acc_sc[idx] = alpha * acc_sc[idx] + _dot(p.astype(BF16), v)
                m_sc[idx] = m_new

    @pl.when(ki < qi)
    def _():
        step(False)

    @pl.when(ki == qi)
    def _():
        step(True)
        lam = _lambda(lq1, lk1, lq2, lk2, lam_init)
        e = 2 * DA_HEAD_DIM
        for r in range(DA_REP):
            o = _subln_out(acc_sc[r * 2], l_sc[r * 2], acc_sc[r * 2 + 1], l_sc[r * 2 + 1],
                           lam, subw_ref[...], lam_init)
            o_ref[:, r * e:(r + 1) * e] = o.astype(o_ref.dtype)


def _attn_prompt(q, kv, subw, lams, lam_init, *, t=512):
    L = q.shape[0]
    t = min(t, L)
    n = L // t
    e = 2 * DA_HEAD_DIM
    vec = pl.BlockSpec((1, DA_HEAD_DIM), lambda g, i, j: (0, 0))
    return pl.pallas_call(
        functools.partial(_attn_prompt_kernel, t=t, lam_init=lam_init),
        grid=(DA_KV_HEADS, n, n),
        in_specs=[pl.BlockSpec((t, DA_REP * e), lambda g, i, j: (i, g)),
                  pl.BlockSpec((t, e), lambda g, i, j: (jnp.minimum(i, j), g)),
                  pl.BlockSpec((t, e), lambda g, i, j: (jnp.minimum(i, j), g + DA_KV_HEADS)),
                  pl.BlockSpec((1, e), lambda g, i, j: (0, 0)),
                  vec, vec, vec, vec],
        out_specs=pl.BlockSpec((t, DA_REP * e), lambda g, i, j: (i, g)),
        out_shape=jax.ShapeDtypeStruct((L, DA_HEADS * e), BF16),
        scratch_shapes=[pltpu.VMEM((2 * DA_REP, t, 1), F32),
                        pltpu.VMEM((2 * DA_REP, t, 1), F32),
                        pltpu.VMEM((2 * DA_REP, t, e), F32)],
        compiler_params=_params("parallel", "parallel", "arbitrary"),
        name="attn_prompt",
    )(q, kv, kv, subw, *lams)


Q_ROWS = 16


def _attn_sample_kernel(pt_ref, q_ref, *rest, pages, page, dec_seq, lam_init):
    k_refs = rest[:pages]
    v_refs = rest[pages:2 * pages]
    kn_ref, vn_ref, subw_ref, lq1, lk1, lq2, lk2, o_ref, m_sc, l_sc, acc_sc = rest[2 * pages:]
    j = pl.program_id(1)
    e = 2 * DA_HEAD_DIM

    @pl.when(j == 0)
    def _():
        m_sc[...] = jnp.full(m_sc.shape, -jnp.inf, F32)
        l_sc[...] = jnp.zeros(l_sc.shape, F32)
        acc_sc[...] = jnp.zeros(acc_sc.shape, F32)

    def update(k_list, v_list, keep):
        for g in range(DA_KV_HEADS):
            vg = [vr[0, :, g * e:(g + 1) * e].astype(BF16) for vr in v_list]
            for c in range(2):
                idx = g * 2 + c
                q = q_ref[0, idx]
                s = jnp.concatenate(
                    [_dot(q, kr[0, :, idx * DA_HEAD_DIM:(idx + 1) * DA_HEAD_DIM].astype(BF16), _NT)
                     for kr in k_list], axis=1)
                if keep is not None:
                    s = jnp.where(keep, s, NEG_INF)
                m_prev = m_sc[idx]
                m_new = jnp.maximum(m_prev, jnp.max(s, axis=-1, keepdims=True))
                alpha = jnp.exp(m_prev - m_new)
                p = jnp.exp(s - m_new)
                l_sc[idx] = alpha * l_sc[idx] + jnp.sum(p, axis=-1, keepdims=True)
                pv = _dot(p[:, :page].astype(BF16), vg[0])
                for i in range(1, len(v_list)):
                    pv = pv + _dot(p[:, i * page:(i + 1) * page].astype(BF16), vg[i])
                acc_sc[idx] = alpha * acc_sc[idx] + pv
                m_sc[idx] = m_new

    update(k_refs, v_refs, None)

    @pl.when(j == pl.num_programs(1) - 1)
    def _():
        row = lax.broadcasted_iota(jnp.int32, (Q_ROWS, page), 0)
        col = lax.broadcasted_iota(jnp.int32, (Q_ROWS, page), 1)
        update([kn_ref], [vn_ref], col <= row % dec_seq)
        lam = _lambda(lq1, lk1, lq2, lk2, lam_init)
        for g in range(DA_KV_HEADS):
            o = _subln_out(acc_sc[g * 2], l_sc[g * 2], acc_sc[g * 2 + 1], l_sc[g * 2 + 1],
                           lam, subw_ref[...], lam_init)
            o_ref[0, g] = o.astype(o_ref.dtype)


def _attn_sample(q, cache_k, cache_v, page_table, k_new, v_new, subw, lams, lam_init, dec_seq):
    B = q.shape[0]
    n_pages = page_table.shape[1]
    page = cache_k.shape[1]
    width = cache_k.shape[2]
    pages = math.gcd(n_pages, 8)
    e = 2 * DA_HEAD_DIM

    def page_spec(i):
        return pl.BlockSpec((1, page, width), lambda b, j, pt: (pt[b, j * pages + i], 0, 0))

    vec = pl.BlockSpec((1, DA_HEAD_DIM), lambda b, j, pt: (0, 0))
    new_spec = pl.BlockSpec((1, page, width), lambda b, j, pt: (b, 0, 0))
    grid_spec = pltpu.PrefetchScalarGridSpec(
        num_scalar_prefetch=1,
        grid=(B, n_pages // pages),
        in_specs=([pl.BlockSpec((1, 2 * DA_KV_HEADS, Q_ROWS, DA_HEAD_DIM), lambda b, j, pt: (b, 0, 0, 0))]
                  + [page_spec(i) for i in range(pages)] + [page_spec(i) for i in range(pages)]
                  + [new_spec, new_spec, pl.BlockSpec((1, e), lambda b, j, pt: (0, 0)),
                     vec, vec, vec, vec]),
        out_specs=pl.BlockSpec((1, DA_KV_HEADS, Q_ROWS, e), lambda b, j, pt: (b, 0, 0, 0)),
        scratch_shapes=[pltpu.VMEM((2 * DA_KV_HEADS, Q_ROWS, 1), F32),
                        pltpu.VMEM((2 * DA_KV_HEADS, Q_ROWS, 1), F32),
                        pltpu.VMEM((2 * DA_KV_HEADS, Q_ROWS, e), F32)])
    return pl.pallas_call(
        functools.partial(_attn_sample_kernel, pages=pages, page=page, dec_seq=dec_seq,
                          lam_init=lam_init),
        grid_spec=grid_spec,
        out_shape=jax.ShapeDtypeStruct((B, DA_KV_HEADS, Q_ROWS, e), F32),
        compiler_params=_params("parallel", "arbitrary"),
        name="attn_sample",
    )(page_table, q, *([cache_k] * pages), *([cache_v] * pages), k_new, v_new, subw, *lams)


HALO = 8


def _conv_kernel(x_ref, w_ref, o_ref, xs_sc, *, tm, tc, normalize, scale):
    i = pl.program_id(1)

    @pl.when(i == 0)
    def _():
        xs_sc[0:HALO, :] = jnp.zeros((HALO, tc), F32)

    xs_sc[HALO:HALO + tm, :] = x_ref[...]
    y = w_ref[CONV_W - 1:CONV_W, :] * xs_sc[HALO:HALO + tm, :]
    for back in range(1, CONV_W):
        y = y + w_ref[CONV_W - 1 - back:CONV_W - back, :] * xs_sc[HALO - back:HALO - back + tm, :]
    y = y * _sigmoid(y)
    if normalize:
        for h in range(tc // GD_DIM):
            ys = y[:, h * GD_DIM:(h + 1) * GD_DIM]
            ss = jnp.sum(ys * ys, axis=-1, keepdims=True)
            o_ref[:, h * GD_DIM:(h + 1) * GD_DIM] = ys * (lax.rsqrt(ss + L2_EPS) * scale)
    else:
        o_ref[...] = y
    xs_sc[0:HALO, :] = xs_sc[tm:tm + HALO, :]


def _conv_silu(x, w, col0, ncols, *, normalize, scale=1.0, tm=512, tc=512):
    L = x.shape[0]
    tm = min(tm, L)
    off = col0 // tc
    return pl.pallas_call(
        functools.partial(_conv_kernel, tm=tm, tc=tc, normalize=normalize, scale=scale),
        grid=(ncols // tc, L // tm),
        in_specs=[pl.BlockSpec((tm, tc), lambda c, i: (i, c + off)),
                  pl.BlockSpec((CONV_W, tc), lambda c, i: (0, c + off))],
        out_specs=pl.BlockSpec((tm, tc), lambda c, i: (i, c)),
        out_shape=jax.ShapeDtypeStruct((L, ncols), F32),
        scratch_shapes=[pltpu.VMEM((tm + HALO, tc), F32)],
        compiler_params=_params("parallel", "arbitrary"),
        name="conv_silu",
    )(x, w)


def _gdn_intra_kernel(q_ref, k_ref, v_ref, arow_ref, acol_ref, bcol_ref, alog_ref, dtb_ref,
                      u_ref, wq_ref, qk_ref, ket_ref, egl_ref, *, nb, C):
    neg_a = -jnp.exp(alog_ref[0])
    dtb = dtb_ref[0]
    row = lax.broadcasted_iota(jnp.int32, (C, C), 0)
    col = lax.broadcasted_iota(jnp.int32, (C, C), 1)
    tril = col <= row
    strict = col < row
    eye = (col == row).astype(F32)
    for c in range(nb):
        rows = slice(c * C, (c + 1) * C)
        q = q_ref[rows, :]
        k = k_ref[rows, :]
        v = v_ref[rows, :]
        g_row = neg_a * _softplus(arow_ref[0, c] + dtb)
        g_col = neg_a * _softplus(acol_ref[0, c] + dtb)
        beta = _sigmoid(bcol_ref[0, c])
        gc_col = jnp.sum(jnp.where(tril, g_row, 0.0), axis=1, keepdims=True)
        gc_row = jnp.sum(jnp.where(row <= col, g_col, 0.0), axis=0, keepdims=True)
        decay = jnp.where(tril, jnp.exp(jnp.where(tril, gc_col - gc_row, 0.0)), 0.0)
        kb = k * beta
        vb = v * beta
        lm = jnp.where(strict, _dot3(kb, k, _NT) * decay, 0.0)
        qk_ref[0, c] = jnp.where(tril, _dot3(q, k, _NT) * decay, 0.0)
        x = eye - lm
        p = lm
        n = 1
        while 2 * n < C:
            p = _dot3(p, p)
            x = x + _dot3(x, p)
            n *= 2
        eg = jnp.exp(gc_col)
        uw = _dot3(x, jnp.concatenate([vb, kb * eg], axis=1))
        u_ref[0, c] = uw[:, :GD_DIM]
        wq_ref[0, c, 0:C, :] = uw[:, GD_DIM:]
        wq_ref[0, c, C:2 * C, :] = q * eg
        g_last = gc_col[C - 1:C, :]
        ket_ref[0, c] = (k * jnp.exp(g_last - gc_col)).T
        egl_ref[0, c] = jnp.broadcast_to(jnp.exp(g_last), (1, GD_DIM))


def _gdn_scan_kernel(u_ref, wq_ref, qk_ref, ket_ref, egl_ref, z_ref, gw_ref, o_ref, s_ref,
                     s_sc, *, nb, C):
    n = pl.program_id(1)

    @pl.when(n == 0)
    def _():
        s_sc[...] = jnp.zeros(s_sc.shape, F32)

    for c in range(nb):
        s = s_sc[...]
        r = _dot3(wq_ref[0, c], s)
        v_new = u_ref[0, c] - r[:C]
        o = r[C:] + _dot3(qk_ref[0, c], v_new)
        s_sc[...] = s * egl_ref[0, c] + _dot3(ket_ref[0, c], v_new)
        rows = slice(c * C, (c + 1) * C)
        z = z_ref[rows, :]
        o = o * lax.rsqrt(jnp.mean(o * o, axis=-1, keepdims=True) + NORM_EPS) * gw_ref[...]
        o_ref[rows, :] = (o * (z * _sigmoid(z))).astype(o_ref.dtype)

    @pl.when(n == pl.num_programs(1) - 1)
    def _():
        s_ref[0] = s_sc[...]


def _gdn_prompt(qn, kn, vv, z, a, b, a_log, dt_bias, gnorm_w, *, nb=8):
    L = qn.shape[0]
    H = GD_V_HEADS
    C = min(CHUNK, L)
    N = L // C
    nb = math.gcd(nb, N)
    a_t = a.T.reshape(H, N, C)
    b_t = b.T.reshape(H, N, C)
    hvec = lambda x: x.reshape(H, 1, 1)
    blk = lambda *shape: pl.BlockSpec((1, nb) + shape, lambda h, n: (h, n, 0, 0))
    u, wq, qk, ket, egl = pl.pallas_call(
        functools.partial(_gdn_intra_kernel, nb=nb, C=C),
        grid=(H, N // nb),
        in_specs=[pl.BlockSpec((nb * C, GD_DIM), lambda h, n: (n, h // GD_REP)),
                  pl.BlockSpec((nb * C, GD_DIM), lambda h, n: (n, h // GD_REP)),
                  pl.BlockSpec((nb * C, GD_DIM), lambda h, n: (n, h)),
                  blk(1, C), blk(C, 1), blk(C, 1),
                  pl.BlockSpec((1, 1, 1), lambda h, n: (h, 0, 0)),
                  pl.BlockSpec((1, 1, 1), lambda h, n: (h, 0, 0))],
        out_specs=[blk(C, GD_DIM), blk(2 * C, GD_DIM), blk(C, C), blk(GD_DIM, C), blk(1, GD_DIM)],
        out_shape=[jax.ShapeDtypeStruct((H, N, C, GD_DIM), F32),
                   jax.ShapeDtypeStruct((H, N, 2 * C, GD_DIM), F32),
                   jax.ShapeDtypeStruct((H, N, C, C), F32),
                   jax.ShapeDtypeStruct((H, N, GD_DIM, C), F32),
                   jax.ShapeDtypeStruct((H, N, 1, GD_DIM), F32)],
        compiler_params=_params("parallel", "parallel"),
        name="gdn_intra",
    )(qn, kn, vv, a_t[:, :, None, :], a_t[:, :, :, None], b_t[:, :, :, None],
      hvec(a_log), hvec(dt_bias))
    out, state = pl.pallas_call(
        functools.partial(_gdn_scan_kernel, nb=nb, C=C),
        grid=(H, N // nb),
        in_specs=[blk(C, GD_DIM), blk(2 * C, GD_DIM), blk(C, C), blk(GD_DIM, C), blk(1, GD_DIM),
                  pl.BlockSpec((nb * C, GD_DIM), lambda h, n: (n, h)),
                  pl.BlockSpec((1, GD_DIM), lambda h, n: (0, 0))],
        out_specs=[pl.BlockSpec((nb * C, GD_DIM), lambda h, n: (n, h)),
                   pl.BlockSpec((1, GD_DIM, GD_DIM), lambda h, n: (h, 0, 0))],
        out_shape=[jax.ShapeDtypeStruct((L, H * GD_DIM), BF16),
                   jax.ShapeDtypeStruct((H, GD_DIM, GD_DIM), F32)],
        scratch_shapes=[pltpu.VMEM((GD_DIM, GD_DIM), F32)],
        compiler_params=_params("parallel", "arbitrary"),
        name="gdn_scan",
    )(u, wq, qk, ket, egl, z, gnorm_w)
    return out, state


def _gdn_sample_kernel(qt_ref, kt_ref, v_ref, z_ref, a_ref, b_ref, alog_ref, dtb_ref, gw_ref,
                       s_ref, o_ref, so_ref, *, T):
    def head(h, carry):
        hk = h // GD_REP
        s = s_ref[0, h]
        qt = qt_ref[0, hk]
        kt = kt_ref[0, hk]
        g = -jnp.exp(alog_ref[h]) * _softplus(a_ref[0, h] + dtb_ref[h])
        beta = _sigmoid(b_ref[0, h])
        v = v_ref[0, h]
        outs = []
        for t in range(T):
            kc = kt[:, t:t + 1]
            s = s * jnp.exp(g[t:t + 1, :])
            v_new = beta[t:t + 1, :] * (v[t:t + 1, :] - jnp.sum(kc * s, axis=0, keepdims=True))
            s = s + kc * v_new
            outs.append(jnp.sum(qt[:, t:t + 1] * s, axis=0, keepdims=True))
        so_ref[0, h] = s
        o = jnp.concatenate(outs, axis=0)
        z = z_ref[0, h]
        o = o * lax.rsqrt(jnp.mean(o * o, axis=-1, keepdims=True) + NORM_EPS) * gw_ref[...]
        o_ref[0, h] = o * (z * _sigmoid(z))
        return carry

    lax.fori_loop(0, GD_V_HEADS, head, 0)


def _gdn_sample(qt, kt, v, z, a, b, a_log, dt_bias, gnorm_w, state):
    B, H, T, _ = v.shape
    per_b = lambda *shape: pl.BlockSpec((1,) + shape, lambda i: (i,) + (0,) * len(shape))
    hpar = pl.BlockSpec((H, 1, GD_DIM), lambda i: (0, 0, 0))
    return pl.pallas_call(
        functools.partial(_gdn_sample_kernel, T=T),
        grid=(B,),
        in_specs=[per_b(GD_K_HEADS, GD_DIM, T), per_b(GD_K_HEADS, GD_DIM, T),
                  per_b(H, T, GD_DIM), per_b(H, T, GD_DIM), per_b(H, T, GD_DIM), per_b(H, T, GD_DIM),
                  hpar, hpar, pl.BlockSpec((1, GD_DIM), lambda i: (0, 0)),
                  per_b(H, GD_DIM, GD_DIM)],
        out_specs=[per_b(H, T, GD_DIM), per_b(H, GD_DIM, GD_DIM)],
        out_shape=[jax.ShapeDtypeStruct((B, H, T, GD_DIM), F32),
                   jax.ShapeDtypeStruct((B, H, GD_DIM, GD_DIM), F32)],
        compiler_params=_params("parallel"),
        name="gdn_sample",
    )(qt, kt, v, z, a, b, a_log, dt_bias, gnorm_w, state)


def _merge_kernel(oa_ref, ob_ref, wa_ref, wb_ref, ga_ref, gb_ref, o_ref):
    ya = _dot(oa_ref[...], wa_ref[...])
    yb = _dot(ob_ref[...], wb_ref[...])
    o_ref[...] = (_sigmoid(ga_ref[...]) * ya + _sigmoid(gb_ref[...]) * yb).astype(o_ref.dtype)


def _merge(oa, ob, wa, wb, gates, *, tm=512, tn=512):
    M = oa.shape[0]
    D = wa.shape[1]
    tm = min(tm, M)
    tn = min(tn, D)
    nj = D // tn
    return pl.pallas_call(
        _merge_kernel,
        grid=(nj, M // tm),
        in_specs=[pl.BlockSpec((tm, oa.shape[1]), lambda j, i: (i, 0)),
                  pl.BlockSpec((tm, ob.shape[1]), lambda j, i: (i, 0)),
                  pl.BlockSpec((wa.shape[0], tn), lambda j, i: (0, j)),
                  pl.BlockSpec((wb.shape[0], tn), lambda j, i: (0, j)),
                  pl.BlockSpec((tm, tn), lambda j, i: (i, j)),
                  pl.BlockSpec((tm, tn), lambda j, i: (i, j + nj))],
        out_specs=pl.BlockSpec((tm, tn), lambda j, i: (i, j)),
        out_shape=jax.ShapeDtypeStruct((M, D), BF16),
        compiler_params=_params("parallel", "parallel"),
        name="merge",
    )(oa, ob, wa, wb, gates, gates)


def _outproj_kernel(m_ref, w_ref, x_ref, g_ref, b_ref, o_ref, *, alpha):
    mix = _dot(m_ref[...], w_ref[...])
    o_ref[...] = _layer_norm(alpha * x_ref[...] + mix, g_ref[...], b_ref[...])


def _outproj_ln(merged, w, x, g, b, alpha, *, tm=256):
    M, D = x.shape
    tm = min(tm, M)
    row = pl.BlockSpec((1, D), lambda i: (0, 0))
    return pl.pallas_call(
        functools.partial(_outproj_kernel, alpha=alpha),
        grid=(M // tm,),
        in_specs=[pl.BlockSpec((tm, D), lambda i: (i, 0)),
                  pl.BlockSpec((D, D), lambda i: (0, 0)),
                  pl.BlockSpec((tm, D), lambda i: (i, 0)), row, row],
        out_specs=pl.BlockSpec((tm, D), lambda i: (i, 0)),
        out_shape=jax.ShapeDtypeStruct((M, D), F32),
        compiler_params=_params("parallel"),
        name="outproj_ln",
    )(merged, w, x, g, b)


def _ffn_kernel(h_ref, wu_ref, wd_ref, g_ref, b_ref, o_ref, hb_sc, acc_sc, *, alpha):
    f = pl.program_id(1)

    @pl.when(f == 0)
    def _():
        hb_sc[...] = h_ref[...].astype(BF16)
        acc_sc[...] = jnp.zeros(acc_sc.shape, F32)

    u = jnp.maximum(_dot(hb_sc[...], wu_ref[...]), 0.0)
    acc_sc[...] += _dot((u * u).astype(BF16), wd_ref[...])

    @pl.when(f == pl.num_programs(1) - 1)
    def _():
        o_ref[...] = _layer_norm(alpha * h_ref[...] + acc_sc[...], g_ref[...], b_ref[...])


def _ffn_ln(h, wu, wd, g, b, alpha, *, tm=512, tf=512):
    M, D = h.shape
    F = wu.shape[1]
    tm = min(tm, M)
    tf = min(tf, F)
    row = pl.BlockSpec((1, D), lambda i, f: (0, 0))
    return pl.pallas_call(
        functools.partial(_ffn_kernel, alpha=alpha),
        grid=(M // tm, F // tf),
        in_specs=[pl.BlockSpec((tm, D), lambda i, f: (i, 0)),
                  pl.BlockSpec((D, tf), lambda i, f: (0, f)),
                  pl.BlockSpec((tf, D), lambda i, f: (f, 0)), row, row],
        out_specs=pl.BlockSpec((tm, D), lambda i, f: (i, 0)),
        out_shape=jax.ShapeDtypeStruct((M, D), F32),
        scratch_shapes=[pltpu.VMEM((tm, D), BF16), pltpu.VMEM((tm, D), F32)],
        compiler_params=_params("parallel", "arbitrary"),
        name="ffn_ln",
    )(h, wu, wd, g, b)


def _lambda_init(layer):
    return 0.8 - 0.6 * math.exp(-0.3 * layer)


def _in_projection(x, wts):
    xb = x.astype(BF16)
    (q,) = _matmul(xb, wts["w_q"], [BF16], scale=DA_HEAD_DIM ** -0.5)
    kv, kv_b = _matmul(xb, wts["w_kv"], [F32, BF16])
    (xc,) = _matmul(xb, wts["w_gd"], [F32])
    (z,) = _matmul(xb, wts["w_z"], [F32])
    (ab,) = _matmul(xb, wts["w_ab"], [F32])
    (gates,) = _matmul(xb, wts["w_gates"], [F32])
    return q, kv, kv_b, xc, z, ab, gates


def _tail(x, oa, ob, gates, wts, alpha):
    merged = _merge(oa, ob, wts["w_proj_a"], wts["w_proj_b"], gates)
    h = _outproj_ln(merged, wts["w_out"], x, wts["ln1_g"], wts["ln1_b"], alpha)
    return _ffn_ln(h, wts["w_up"], wts["w_down"], wts["ln2_g"], wts["ln2_b"], alpha)


def _layer_prompt(x, wts, lam_init, alpha):
    L = x.shape[0]
    nkv = DA_KV_HEADS * 2 * DA_HEAD_DIM
    nk = GD_K_HEADS * GD_DIM
    q, kv, kv_b, xc, z, ab, gates = _in_projection(x, wts)
    oa = _attn_prompt(q, kv_b, wts["subln_w"], wts["lams"], lam_init)
    qn = _conv_silu(xc, wts["conv_w"], 0, nk, normalize=True, scale=GD_DIM ** -0.5)
    kn = _conv_silu(xc, wts["conv_w"], nk, nk, normalize=True)
    vv = _conv_silu(xc, wts["conv_w"], 2 * nk, GD_V_HEADS * GD_DIM, normalize=False)
    ob, state = _gdn_prompt(qn, kn, vv, z, ab[:, :GD_V_HEADS], ab[:, GD_V_HEADS:],
                            wts["a_log"], wts["dt_bias"], wts["gnorm_w"])
    y = _tail(x, oa, ob, gates, wts, alpha)
    return y, kv[:, :nkv], kv[:, nkv:], state, xc[L - (CONV_W - 1):]


def _layer_sample(x, cache_k, cache_v, page_table, state_ssm, state_conv, wts, lam_init, alpha):
    B, T, D = x.shape
    M = B * T
    H = GD_V_HEADS
    nkv = DA_KV_HEADS * 2 * DA_HEAD_DIM
    nk = GD_K_HEADS * GD_DIM
    page = cache_k.shape[1]
    x2 = x.reshape(M, D)
    q, kv, _, xc, z, ab, gates = _in_projection(x2, wts)
    qh = q.reshape(B, T, DA_KV_HEADS, DA_REP, 2, DA_HEAD_DIM).transpose(0, 2, 4, 3, 1, 5)
    qh = qh.reshape(B, 2 * DA_KV_HEADS, DA_REP * T, DA_HEAD_DIM)
    qh = jnp.pad(qh, ((0, 0), (0, 0), (0, Q_ROWS - DA_REP * T), (0, 0)))
    k_new = jnp.pad(kv[:, :nkv].reshape(B, T, nkv), ((0, 0), (0, page - T), (0, 0)))
    v_new = jnp.pad(kv[:, nkv:].reshape(B, T, nkv), ((0, 0), (0, page - T), (0, 0)))
    oa = _attn_sample(qh, cache_k, cache_v, page_table, k_new, v_new, wts["subln_w"], wts["lams"],
                      lam_init, T)
    oa = oa[:, :, :DA_REP * T].reshape(B, DA_KV_HEADS, DA_REP, T, 2 * DA_HEAD_DIM)
    oa = oa.transpose(0, 3, 1, 2, 4).reshape(M, DA_HEADS * 2 * DA_HEAD_DIM).astype(BF16)
    cdim = xc.shape[1]
    xc3 = xc.reshape(B, T, cdim)
    grp = jnp.concatenate([jnp.zeros((B, HALO - (CONV_W - 1) - T, cdim), F32), state_conv, xc3], axis=1)
    grp = grp.reshape(B * HALO, cdim)
    take = lambda y: y.reshape(B, HALO, -1)[:, HALO - T:]
    qn = take(_conv_silu(grp, wts["conv_w"], 0, nk, normalize=True, scale=GD_DIM ** -0.5))
    kn = take(_conv_silu(grp, wts["conv_w"], nk, nk, normalize=True))
    vv = take(_conv_silu(grp, wts["conv_w"], 2 * nk, H * GD_DIM, normalize=False))
    to_cols = lambda y: y.reshape(B, T, GD_K_HEADS, GD_DIM).transpose(0, 2, 3, 1)
    to_heads = lambda y: y.reshape(B, T, H, GD_DIM).transpose(0, 2, 1, 3)
    lanes = lambda y: jnp.broadcast_to(y.reshape(B, T, H).transpose(0, 2, 1)[..., None], (B, H, T, GD_DIM))
    hpar = lambda p: jnp.broadcast_to(p.reshape(H, 1, 1), (H, 1, GD_DIM))
    ob, state = _gdn_sample(to_cols(qn), to_cols(kn), to_heads(vv), to_heads(z),
                            lanes(ab[:, :H]), lanes(ab[:, H:]), hpar(wts["a_log"]),
                            hpar(wts["dt_bias"]), wts["gnorm_w"], state_ssm)
    ob = ob.transpose(0, 2, 1, 3).reshape(M, H * GD_DIM).astype(BF16)
    y = _tail(x2, oa, ob, gates, wts, alpha)
    return (y.reshape(B, T, D), kv[:, :nkv].reshape(B, T, nkv), kv[:, nkv:].reshape(B, T, nkv),
            state, xc3[:, T - (CONV_W - 1):])


def _layer_weights(l, w_in, lambda_q1, lambda_k1, lambda_q2, lambda_k2, subln_w, conv_w, a_log,
                   dt_bias, gnorm_w, w_proj_a, w_proj_b, w_out, ln1_g, ln1_b, w_up, w_down,
                   ln2_g, ln2_b):
    nq = DA_HEADS * 2 * DA_HEAD_DIM
    nkv = DA_KV_HEADS * 2 * DA_HEAD_DIM
    nk = GD_K_HEADS * GD_DIM
    nv = GD_V_HEADS * GD_DIM
    b0 = nq
    b1 = b0 + 2 * nkv
    b2 = b1 + 2 * nk + nv
    b3 = b2 + nv
    b4 = b3 + 2 * GD_V_HEADS
    w = w_in[l]
    row = lambda p: p[l].reshape(1, -1)
    return {
        "w_q": w[:, :b0].astype(BF16), "w_kv": w[:, b0:b1].astype(BF16),
        "w_gd": w[:, b1:b2].astype(BF16), "w_z": w[:, b2:b3].astype(BF16),
        "w_ab": w[:, b3:b4].astype(BF16), "w_gates": w[:, b4:].astype(BF16),
        "lams": (row(lambda_q1), row(lambda_k1), row(lambda_q2), row(lambda_k2)),
        "subln_w": row(subln_w), "conv_w": conv_w[l], "a_log": a_log[l], "dt_bias": dt_bias[l],
        "gnorm_w": row(gnorm_w),
        "w_proj_a": w_proj_a[l].astype(BF16), "w_proj_b": w_proj_b[l].astype(BF16),
        "w_out": w_out[l].astype(BF16), "ln1_g": row(ln1_g), "ln1_b": row(ln1_b),
        "w_up": w_up[l].astype(BF16), "w_down": w_down[l].astype(BF16),
        "ln2_g": row(ln2_g), "ln2_b": row(ln2_b),
    }


def kernel(x_prompt, x_sample, cache_k, cache_v, page_table, state_ssm, state_conv, w_in, lambda_q1, lambda_k1, lambda_q2, lambda_k2, subln_w, conv_w, a_log, dt_bias, gnorm_w, w_proj_a, w_proj_b, w_out, ln1_g, ln1_b, w_up, w_down, ln2_g, ln2_b):
    depth = w_in.shape[0]
    alpha = (2.0 * depth) ** 0.25
    bp, seq, d = x_prompt.shape
    n_pool, page = cache_k.shape[1], cache_k.shape[2]
    nkv = DA_KV_HEADS * 2 * DA_HEAD_DIM
    yp, ys = x_prompt, x_sample
    outs = [[] for _ in range(8)]
    for l in range(depth):
        wts = _layer_weights(l, w_in, lambda_q1, lambda_k1, lambda_q2, lambda_k2, subln_w, conv_w,
                             a_log, dt_bias, gnorm_w, w_proj_a, w_proj_b, w_out, ln1_g, ln1_b,
                             w_up, w_down, ln2_g, ln2_b)
        lam_init = _lambda_init(l)
        per_seq = [_layer_prompt(yp[i], wts, lam_init, alpha) for i in range(bp)]
        yp, kp, vp, sp, cp = (jnp.stack(t) for t in zip(*per_seq))
        ys, k_s, v_s, s_s, c_s = _layer_sample(
            ys, cache_k[l].reshape(n_pool, page, nkv), cache_v[l].reshape(n_pool, page, nkv),
            page_table, state_ssm[l], state_conv[l], wts, lam_init, alpha)
        kv_shape = (DA_KV_HEADS, 2, DA_HEAD_DIM)
        for lst, val in zip(outs, (kp.reshape(kp.shape[:2] + kv_shape), vp.reshape(vp.shape[:2] + (DA_KV_HEADS, -1)),
                                   sp, cp, k_s.reshape(k_s.shape[:2] + kv_shape),
                                   v_s.reshape(v_s.shape[:2] + (DA_KV_HEADS, -1)), s_s, c_s)):
            lst.append(val)
    return (yp, ys) + tuple(jnp.stack(o) for o in outs)
```

```python
import functools
import math

import jax
import jax.numpy as jnp
from jax import lax
from jax.experimental import pallas as pl
from jax.experimental.pallas import tpu as pltpu

F32 = jnp.float32
BF16 = jnp.bfloat16

DA_HEADS = 8
DA_KV_HEADS = 4
DA_HEAD_DIM = 128
DA_REP = DA_HEADS // DA_KV_HEADS
GD_K_HEADS = 16
GD_V_HEADS = 32
GD_DIM = 128
GD_REP = GD_V_HEADS // GD_K_HEADS
CONV_W = 4
CHUNK = 64
NORM_EPS = 1e-5
L2_EPS = 1e-6
NEG_INF = -1e30

V7X_VMEM_BYTES = 64 * 1024 * 1024
VMEM_LIMIT = V7X_VMEM_BYTES - 12 * 1024 * 1024

_NT = (((1,), (1,)), ((), ()))
_NN = (((1,), (0,)), ((), ()))


def _params(*sem):
    return pltpu.CompilerParams(dimension_semantics=sem, vmem_limit_bytes=VMEM_LIMIT)


def _dot(a, b, dn=_NN):
    return lax.dot_general(a, b, dn, preferred_element_type=F32)


def _split(a):
    hi = a.astype(BF16)
    lo = (a - hi.astype(F32)).astype(BF16)
    return hi, lo


def _dot3(a, b, dn=_NN):
    ah, al = _split(a)
    bh, bl = _split(b)
    return _dot(ah, bh, dn) + (_dot(al, bh, dn) + _dot(ah, bl, dn))


def _softplus(x):
    return jnp.maximum(x, 0.0) + jnp.log(1.0 + jnp.exp(-jnp.abs(x)))


def _sigmoid(x):
    return 1.0 / (1.0 + jnp.exp(-x))


def _lambda(lq1, lk1, lq2, lk2, lam_init):
    a = jnp.sum(lq1[...] * lk1[...], axis=-1, keepdims=True)
    b = jnp.sum(lq2[...] * lk2[...], axis=-1, keepdims=True)
    return jnp.exp(a) - jnp.exp(b) + lam_init


def _layer_norm(x, g, b):
    mu = jnp.mean(x, axis=-1, keepdims=True)
    xc = x - mu
    var = jnp.mean(xc * xc, axis=-1, keepdims=True)
    return xc * lax.rsqrt(var + NORM_EPS) * g + b


def _mm_kernel(x_ref, w_ref, *o_refs, scale):
    acc = _dot(x_ref[...], w_ref[...])
    if scale != 1.0:
        acc = acc * scale
    for o in o_refs:
        o[...] = acc.astype(o.dtype)


def _matmul(x, w, out_dtypes, *, scale=1.0, tm=1024, tn=1024):
    M, K = x.shape
    N = w.shape[1]
    tm = min(tm, M)
    tn = min(tn, N)
    outs = pl.pallas_call(
        functools.partial(_mm_kernel, scale=scale),
        grid=(N // tn, M // tm),
        in_specs=[pl.BlockSpec((tm, K), lambda j, i: (i, 0)),
                  pl.BlockSpec((K, tn), lambda j, i: (0, j))],
        out_specs=[pl.BlockSpec((tm, tn), lambda j, i: (i, j)) for _ in out_dtypes],
        out_shape=[jax.ShapeDtypeStruct((M, N), d) for d in out_dtypes],
        compiler_params=_params("parallel", "parallel"),
        name="in_proj",
    )(x, w)
    return outs


def _subln_out(acc0, l0, acc1, l1, lam, subw, lam_init):
    o = acc0 / l0 - lam * (acc1 / l1)
    o = o * lax.rsqrt(jnp.mean(o * o, axis=-1, keepdims=True) + NORM_EPS)
    return o * subw * (1.0 - lam_init)


def _attn_prompt_kernel(q_ref, k_ref, v_ref, subw_ref, lq1, lk1, lq2, lk2, o_ref,
                        m_sc, l_sc, acc_sc, *, tq, tk, lam_init):
    qi = pl.program_id(1)
    ki = pl.program_id(2)
    last = (qi * tq + tq - 1) // tk

    @pl.when(ki == 0)
    def _():
        m_sc[...] = jnp.full(m_sc.shape, -jnp.inf, F32)
        l_sc[...] = jnp.zeros(l_sc.shape, F32)
        acc_sc[...] = jnp.zeros(acc_sc.shape, F32)

    def step(masked):
        v = v_ref[...]
        if masked:
            ahead = (lax.broadcasted_iota(jnp.int32, (tq, tk), 1)
                     - lax.broadcasted_iota(jnp.int32, (tq, tk), 0))
            keep = ahead <= qi * tq - ki * tk
        for r in range(DA_REP):
            for c in range(2):
                idx = r * 2 + c
                q = q_ref[:, idx * DA_HEAD_DIM:(idx + 1) * DA_HEAD_DIM]
                k = k_ref[:, c * DA_HEAD_DIM:(c + 1) * DA_HEAD_DIM]
                s = _dot(q, k, _NT)
                if masked:
                    s = jnp.where(keep, s, NEG_INF)
                m_prev = m_sc[idx]
                m_new = jnp.maximum(m_prev, jnp.max(s, axis=-1, keepdims=True))
                alpha = jnp.exp2(m_prev - m_new)
                p = jnp.exp2(s - m_new)
                l_sc[idx] = alpha * l_sc[idx] + jnp.sum(p, axis=-1, keepdims=True)
                acc_sc[idx] = alpha * acc_sc[idx] + _dot(p.astype(BF16), v)
                m_sc[idx] = m_new

    @pl.when(ki < last)
    def _():
        step(False)

    @pl.when(ki == last)
    def _():
        step(True)
        lam = _lambda(lq1, lk1, lq2, lk2, lam_init)
        e = 2 * DA_HEAD_DIM
        for r in range(DA_REP):
            o = _subln_out(acc_sc[r * 2], l_sc[r * 2], acc_sc[r * 2 + 1], l_sc[r * 2 + 1],
                           lam, subw_ref[...], lam_init)
            o_ref[:, r * e:(r + 1) * e] = o.astype(o_ref.dtype)


def _attn_prompt(q, kv, subw, lams, lam_init, *, tq=512, tk=1024):
    L = q.shape[0]
    tq = min(tq, L)
    tk = min(tk, L)
    assert tk % tq == 0
    e = 2 * DA_HEAD_DIM
    vec = pl.BlockSpec((1, DA_HEAD_DIM), lambda g, i, j: (0, 0))
    kblk = lambda i, j: jnp.minimum(j, (i * tq + tq - 1) // tk)
    return pl.pallas_call(
        functools.partial(_attn_prompt_kernel, tq=tq, tk=tk, lam_init=lam_init),
        grid=(DA_KV_HEADS, L // tq, L // tk),
        in_specs=[pl.BlockSpec((tq, DA_REP * e), lambda g, i, j: (i, g)),
                  pl.BlockSpec((tk, e), lambda g, i, j: (kblk(i, j), g)),
                  pl.BlockSpec((tk, e), lambda g, i, j: (kblk(i, j), g + DA_KV_HEADS)),
                  pl.BlockSpec((1, e), lambda g, i, j: (0, 0)),
                  vec, vec, vec, vec],
        out_specs=pl.BlockSpec((tq, DA_REP * e), lambda g, i, j: (i, g)),
        out_shape=jax.ShapeDtypeStruct((L, DA_HEADS * e), BF16),
        scratch_shapes=[pltpu.VMEM((2 * DA_REP, tq, 1), F32),
                        pltpu.VMEM((2 * DA_REP, tq, 1), F32),
                        pltpu.VMEM((2 * DA_REP, tq, e), F32)],
        compiler_params=_params("parallel", "parallel", "arbitrary"),
        name="attn_prompt",
    )(q, kv, kv, subw, *lams)


Q_ROWS = 16


PAGE_SLABS = 2 * DA_KV_HEADS


def _attn_sample_kernel(pt_ref, q_ref, *rest, pages, page, dec_seq, lam_init):
    k_refs = rest[:pages]
    v_refs = rest[pages:2 * pages]
    kn_ref, vn_ref, subw_ref, lq1, lk1, lq2, lk2, o_ref, m_sc, l_sc, acc_sc = rest[2 * pages:]
    j = pl.program_id(1)
    rows = 2 * DA_KV_HEADS * Q_ROWS

    @pl.when(j == 0)
    def _():
        m_sc[...] = jnp.full(m_sc.shape, -jnp.inf, F32)
        l_sc[...] = jnp.zeros(l_sc.shape, F32)
        acc_sc[...] = jnp.zeros(acc_sc.shape, F32)

    def slab(ref, i):
        return ref[0, pl.ds(i, page, stride=PAGE_SLABS), :]

    def update(k_list, v_list, keep):
        s = []
        for idx in range(2 * DA_KV_HEADS):
            k = jnp.concatenate([slab(kr, idx) for kr in k_list], axis=0).astype(BF16)
            s.append(_dot(q_ref[0, idx], k, _NT))
        s = jnp.concatenate(s, axis=0)
        if keep is not None:
            s = jnp.where(keep, s, NEG_INF)
        m_prev = m_sc[...]
        m_new = jnp.maximum(m_prev, jnp.max(s, axis=-1, keepdims=True))
        alpha = jnp.exp2(m_prev - m_new)
        p = jnp.exp2(s - m_new)
        l_sc[...] = alpha * l_sc[...] + jnp.sum(p, axis=-1, keepdims=True)
        m_sc[...] = m_new
        p = p.astype(BF16)
        pv = []
        for g in range(DA_KV_HEADS):
            v = jnp.concatenate(
                [jnp.concatenate([slab(vr, g), slab(vr, DA_KV_HEADS + g)], axis=1) for vr in v_list],
                axis=0).astype(BF16)
            pv.append(_dot(p[g * 2 * Q_ROWS:(g + 1) * 2 * Q_ROWS], v))
        acc_sc[...] = alpha * acc_sc[...] + jnp.concatenate(pv, axis=0)

    update(k_refs, v_refs, None)

    @pl.when(j == pl.num_programs(1) - 1)
    def _():
        row = lax.broadcasted_iota(jnp.int32, (rows, page), 0)
        col = lax.broadcasted_iota(jnp.int32, (rows, page), 1)
        update([kn_ref], [vn_ref], col <= (row % Q_ROWS) % dec_seq)
        lam = _lambda(lq1, lk1, lq2, lk2, lam_init)
        for g in range(DA_KV_HEADS):
            r0 = slice(2 * g * Q_ROWS, (2 * g + 1) * Q_ROWS)
            r1 = slice((2 * g + 1) * Q_ROWS, (2 * g + 2) * Q_ROWS)
            o = _subln_out(acc_sc[r0, :], l_sc[r0, :], acc_sc[r1, :], l_sc[r1, :],
                           lam, subw_ref[...], lam_init)
            o_ref[0, g] = o.astype(o_ref.dtype)


def _attn_sample(q, cache_k, cache_v, page_table, k_new, v_new, subw, lams, lam_init, dec_seq):
    B = q.shape[0]
    n_pages = page_table.shape[1]
    page = cache_k.shape[1] // PAGE_SLABS
    pages = math.gcd(n_pages, 8)
    e = 2 * DA_HEAD_DIM
    rows = 2 * DA_KV_HEADS * Q_ROWS

    def page_spec(i):
        return pl.BlockSpec((1, page * PAGE_SLABS, DA_HEAD_DIM),
                            lambda b, j, pt: (pt[b, j * pages + i], 0, 0))

    vec = pl.BlockSpec((1, DA_HEAD_DIM), lambda b, j, pt: (0, 0))
    new_spec = pl.BlockSpec((1, page * PAGE_SLABS, DA_HEAD_DIM), lambda b, j, pt: (b, 0, 0))
    grid_spec = pltpu.PrefetchScalarGridSpec(
        num_scalar_prefetch=1,
        grid=(B, n_pages // pages),
        in_specs=([pl.BlockSpec((1, 2 * DA_KV_HEADS, Q_ROWS, DA_HEAD_DIM), lambda b, j, pt: (b, 0, 0, 0))]
                  + [page_spec(i) for i in range(pages)] + [page_spec(i) for i in range(pages)]
                  + [new_spec, new_spec, pl.BlockSpec((1, e), lambda b, j, pt: (0, 0)),
                     vec, vec, vec, vec]),
        out_specs=pl.BlockSpec((1, DA_KV_HEADS, Q_ROWS, e), lambda b, j, pt: (b, 0, 0, 0)),
        scratch_shapes=[pltpu.VMEM((rows, 1), F32),
                        pltpu.VMEM((rows, 1), F32),
                        pltpu.VMEM((rows, e), F32)])
    return pl.pallas_call(
        functools.partial(_attn_sample_kernel, pages=pages, page=page, dec_seq=dec_seq,
                          lam_init=lam_init),
        grid_spec=grid_spec,
        out_shape=jax.ShapeDtypeStruct((B, DA_KV_HEADS, Q_ROWS, e), F32),
        compiler_params=_params("parallel", "arbitrary"),
        name="attn_sample",
    )(page_table, q, *([cache_k] * pages), *([cache_v] * pages), k_new, v_new, subw, *lams)


HALO = 8


def _conv_kernel(x_ref, w_ref, o_ref, xs_sc, *, tm, tc, normalize, scale):
    i = pl.program_id(1)

    @pl.when(i == 0)
    def _():
        xs_sc[0:HALO, :] = jnp.zeros((HALO, tc), F32)

    xs_sc[HALO:HALO + tm, :] = x_ref[...]
    y = w_ref[CONV_W - 1:CONV_W, :] * xs_sc[HALO:HALO + tm, :]
    for back in range(1, CONV_W):
        y = y + w_ref[CONV_W - 1 - back:CONV_W - back, :] * xs_sc[HALO - back:HALO - back + tm, :]
    y = y * _sigmoid(y)
    if normalize:
        for h in range(tc // GD_DIM):
            ys = y[:, h * GD_DIM:(h + 1) * GD_DIM]
            ss = jnp.sum(ys * ys, axis=-1, keepdims=True)
            o_ref[:, h * GD_DIM:(h + 1) * GD_DIM] = ys * (lax.rsqrt(ss + L2_EPS) * scale)
    else:
        o_ref[...] = y
    xs_sc[0:HALO, :] = xs_sc[tm:tm + HALO, :]


def _conv_silu(x, w, col0, ncols, *, normalize, scale=1.0, tm=512, tc=512):
    L = x.shape[0]
    tm = min(tm, L)
    off = col0 // tc
    return pl.pallas_call(
        functools.partial(_conv_kernel, tm=tm, tc=tc, normalize=normalize, scale=scale),
        grid=(ncols // tc, L // tm),
        in_specs=[pl.BlockSpec((tm, tc), lambda c, i: (i, c + off)),
                  pl.BlockSpec((CONV_W, tc), lambda c, i: (0, c + off))],
        out_specs=pl.BlockSpec((tm, tc), lambda c, i: (i, c)),
        out_shape=jax.ShapeDtypeStruct((L, ncols), F32),
        scratch_shapes=[pltpu.VMEM((tm + HALO, tc), F32)],
        compiler_params=_params("parallel", "arbitrary"),
        name="conv_silu",
    )(x, w)


GROUP_CHUNKS = 4
SCAN_HEADS = 4


def _gdn_intra_kernel(q_ref, k_ref, v_ref, arow_ref, acol_ref, bcol_ref, alog_ref, dtb_ref,
                      mq_ref, bo_ref, egl_ref, *, groups, C, cpg):
    gt = cpg * C
    shift = C.bit_length() - 1
    row = lax.broadcasted_iota(jnp.int32, (gt, gt), 0)
    col = lax.broadcasted_iota(jnp.int32, (gt, gt), 1)
    same = (row >> shift) == (col >> shift)
    tril = same & (col <= row)
    strict = same & (col < row)
    upper = same & (row <= col)
    eye = jnp.where(row == col, 1.0, 0.0)
    r2 = lax.broadcasted_iota(jnp.int32, (cpg * GD_DIM, gt), 0)
    c2 = lax.broadcasted_iota(jnp.int32, (cpg * GD_DIM, gt), 1)
    ke_mask = (r2 >> (GD_DIM.bit_length() - 1)) == (c2 >> shift)
    for gi in range(groups):
        rows = slice(gi * gt, (gi + 1) * gt)
        q = q_ref[rows, :]
        k = k_ref[rows, :]
        k16 = k.astype(BF16)
        raw = _dot(jnp.concatenate([q.astype(BF16), k16], axis=0), k16, _NT)
        for j in range(GD_REP):
            neg_a = -jnp.exp(alog_ref[j])
            dtb = dtb_ref[j]
            g_row = neg_a * _softplus(arow_ref[j, gi] + dtb)
            g_col = neg_a * _softplus(acol_ref[j, gi] + dtb)
            beta = _sigmoid(bcol_ref[j, gi])
            gc_col = jnp.sum(jnp.where(tril, g_row, 0.0), axis=1, keepdims=True)
            gc_row = jnp.sum(jnp.where(upper, g_col, 0.0), axis=0, keepdims=True)
            gl_col = jnp.sum(jnp.where(same, g_row, 0.0), axis=1, keepdims=True)
            decay = jnp.where(tril, jnp.exp(jnp.where(tril, gc_col - gc_row, 0.0)), 0.0)
            lm = jnp.where(strict, beta * raw[gt:] * decay, 0.0)
            qk = raw[:gt] * decay
            x = eye - lm
            p = lm
            n = 1
            while 2 * n < C:
                p16 = p.astype(BF16)
                p = _dot(p16, p16)
                x = x + _dot(x.astype(BF16), p.astype(BF16))
                n *= 2
            eg = jnp.exp(gc_col)
            v = v_ref[rows, j * GD_DIM:(j + 1) * GD_DIM]
            wu = _dot(x.astype(BF16),
                      jnp.concatenate([k * (beta * eg), v * beta], axis=1).astype(BF16))
            ket = (k * jnp.exp(gl_col - gc_col)).T
            ke_bd = jnp.where(ke_mask, jnp.concatenate([ket] * cpg, axis=0), 0.0)
            res = _dot(jnp.concatenate([ke_bd, qk], axis=0).astype(BF16), wu.astype(BF16))
            qe = q * eg
            for c in range(cpg):
                n_idx = gi * cpg + c
                top = slice(c * GD_DIM, (c + 1) * GD_DIM)
                bot = slice(cpg * GD_DIM + c * C, cpg * GD_DIM + (c + 1) * C)
                mq_ref[j, n_idx, 0:GD_DIM, :] = res[top, :GD_DIM].astype(BF16)
                mq_ref[j, n_idx, GD_DIM:GD_DIM + C, :] = (
                    qe[c * C:(c + 1) * C] - res[bot, :GD_DIM]).astype(BF16)
                bo_ref[j, n_idx, 0:GD_DIM, :] = res[top, GD_DIM:]
                bo_ref[j, n_idx, GD_DIM:GD_DIM + C, :] = res[bot, GD_DIM:]
                egl_ref[j, n_idx] = jnp.broadcast_to(jnp.exp(gl_col[c * C:c * C + 1, :]), (1, GD_DIM))


def _gdn_scan_kernel(mq_ref, bo_ref, egl_ref, z_ref, gw_ref, o_ref, s_ref, s_sc, *, nb, C, hb):
    n = pl.program_id(1)

    @pl.when(n == 0)
    def _():
        s_sc[...] = jnp.zeros(s_sc.shape, F32)

    s = [s_sc[h] for h in range(hb)]
    for c in range(nb):
        rows = slice(c * C, (c + 1) * C)
        for h in range(hb):
            cols = slice(h * GD_DIM, (h + 1) * GD_DIM)
            r = _dot(mq_ref[h, c], s[h].astype(BF16))
            s[h] = egl_ref[h, c] * s[h] - r[:GD_DIM] + bo_ref[h, c, 0:GD_DIM, :]
            o = r[GD_DIM:] + bo_ref[h, c, GD_DIM:GD_DIM + C, :]
            z = z_ref[rows, cols]
            o = o * lax.rsqrt(jnp.mean(o * o, axis=-1, keepdims=True) + NORM_EPS) * gw_ref[...]
            o_ref[rows, cols] = (o * (z * _sigmoid(z))).astype(o_ref.dtype)
    for h in range(hb):
        s_sc[h] = s[h]

    @pl.when(n == pl.num_programs(1) - 1)
    def _():
        s_ref[...] = s_sc[...]


def _gdn_prompt(qn, kn, vv, z, a, b, a_log, dt_bias, gnorm_w, *, groups=2):
    L = qn.shape[0]
    H = GD_V_HEADS
    C = min(CHUNK, L)
    assert C & (C - 1) == 0 and L % C == 0
    N = L // C
    cpg = math.gcd(GROUP_CHUNKS, N)
    gt = cpg * C
    ng = N // cpg
    groups = math.gcd(groups, ng)
    nb = groups * cpg
    hb = SCAN_HEADS
    a_t = a.T.reshape(H, ng, gt)
    b_t = b.T.reshape(H, ng, gt)
    hvec = lambda x: x.reshape(H, 1, 1)
    grp = lambda *shape: pl.BlockSpec((GD_REP, groups) + shape, lambda hk, n: (hk, n, 0, 0))
    per_chunk = lambda heads, *shape: pl.BlockSpec((heads, nb) + shape, lambda h, n: (h, n, 0, 0))
    mq, bo, egl = pl.pallas_call(
        functools.partial(_gdn_intra_kernel, groups=groups, C=C, cpg=cpg),
        grid=(GD_K_HEADS, ng // groups),
        in_specs=[pl.BlockSpec((groups * gt, GD_DIM), lambda hk, n: (n, hk)),
                  pl.BlockSpec((groups * gt, GD_DIM), lambda hk, n: (n, hk)),
                  pl.BlockSpec((groups * gt, GD_REP * GD_DIM), lambda hk, n: (n, hk)),
                  grp(1, gt), grp(gt, 1), grp(gt, 1),
                  pl.BlockSpec((GD_REP, 1, 1), lambda hk, n: (hk, 0, 0)),
                  pl.BlockSpec((GD_REP, 1, 1), lambda hk, n: (hk, 0, 0))],
        out_specs=[per_chunk(GD_REP, GD_DIM + C, GD_DIM), per_chunk(GD_REP, GD_DIM + C, GD_DIM),
                   per_chunk(GD_REP, 1, GD_DIM)],
        out_shape=[jax.ShapeDtypeStruct((H, N, GD_DIM + C, GD_DIM), BF16),
                   jax.ShapeDtypeStruct((H, N, GD_DIM + C, GD_DIM), F32),
                   jax.ShapeDtypeStruct((H, N, 1, GD_DIM), F32)],
        compiler_params=_params("parallel", "parallel"),
        name="gdn_intra",
    )(qn, kn, vv, a_t[:, :, None, :], a_t[:, :, :, None], b_t[:, :, :, None],
      hvec(a_log), hvec(dt_bias))
    out, state = pl.pallas_call(
        functools.partial(_gdn_scan_kernel, nb=nb, C=C, hb=hb),
        grid=(H // hb, N // nb),
        in_specs=[per_chunk(hb, GD_DIM + C, GD_DIM), per_chunk(hb, GD_DIM + C, GD_DIM),
                  per_chunk(hb, 1, GD_DIM),
                  pl.BlockSpec((nb * C, hb * GD_DIM), lambda h, n: (n, h)),
                  pl.BlockSpec((1, GD_DIM), lambda h, n: (0, 0))],
        out_specs=[pl.BlockSpec((nb * C, hb * GD_DIM), lambda h, n: (n, h)),
                   pl.BlockSpec((hb, GD_DIM, GD_DIM), lambda h, n: (h, 0, 0))],
        out_shape=[jax.ShapeDtypeStruct((L, H * GD_DIM), BF16),
                   jax.ShapeDtypeStruct((H, GD_DIM, GD_DIM), F32)],
        scratch_shapes=[pltpu.VMEM((hb, GD_DIM, GD_DIM), F32)],
        compiler_params=_params("parallel", "arbitrary"),
        name="gdn_scan",
    )(mq, bo, egl, z, gnorm_w)
    return out, state


def _gdn_sample_kernel(qt_ref, kt_ref, v_ref, z_ref, a_ref, b_ref, alog_ref, dtb_ref, gw_ref,
                       s_ref, o_ref, so_ref, *, T):
    def key_head(hk, carry):
        qt = qt_ref[0, hk]
        kt = kt_ref[0, hk]
        heads = [hk * GD_REP + j for j in range(GD_REP)]
        s = [s_ref[0, h] for h in heads]
        g = [-jnp.exp(alog_ref[h]) * _softplus(a_ref[0, h] + dtb_ref[h]) for h in heads]
        beta = [_sigmoid(b_ref[0, h]) for h in heads]
        v = [v_ref[0, h] for h in heads]
        outs = [[] for _ in heads]
        for t in range(T):
            kc = kt[:, t:t + 1]
            qc = qt[:, t:t + 1]
            for j in range(GD_REP):
                sj = s[j] * jnp.exp(g[j][t:t + 1, :])
                v_new = beta[j][t:t + 1, :] * (v[j][t:t + 1, :] - jnp.sum(kc * sj, axis=0, keepdims=True))
                s[j] = sj + kc * v_new
                outs[j].append(jnp.sum(qc * s[j], axis=0, keepdims=True))
        for j, h in enumerate(heads):
            so_ref[0, h] = s[j]
            o = jnp.concatenate(outs[j], axis=0)
            z = z_ref[0, h]
            o = o * lax.rsqrt(jnp.mean(o * o, axis=-1, keepdims=True) + NORM_EPS) * gw_ref[...]
            o_ref[0, h] = o * (z * _sigmoid(z))
        return carry

    lax.fori_loop(0, GD_K_HEADS, key_head, 0)


def _gdn_sample(qt, kt, v, z, a, b, a_log, dt_bias, gnorm_w, state):
    B, H, T, _ = v.shape
    per_b = lambda *shape: pl.BlockSpec((1,) + shape, lambda i: (i,) + (0,) * len(shape))
    hpar = pl.BlockSpec((H, 1, GD_DIM), lambda i: (0, 0, 0))
    return pl.pallas_call(
        functools.partial(_gdn_sample_kernel, T=T),
        grid=(B,),
        in_specs=[per_b(GD_K_HEADS, GD_DIM, T), per_b(GD_K_HEADS, GD_DIM, T),
                  per_b(H, T, GD_DIM), per_b(H, T, GD_DIM), per_b(H, T, GD_DIM), per_b(H, T, GD_DIM),
                  hpar, hpar, pl.BlockSpec((1, GD_DIM), lambda i: (0, 0)),
                  per_b(H, GD_DIM, GD_DIM)],
        out_specs=[per_b(H, T, GD_DIM), per_b(H, GD_DIM, GD_DIM)],
        out_shape=[jax.ShapeDtypeStruct((B, H, T, GD_DIM), F32),
                   jax.ShapeDtypeStruct((B, H, GD_DIM, GD_DIM), F32)],
        compiler_params=_params("parallel"),
        name="gdn_sample",
    )(qt, kt, v, z, a, b, a_log, dt_bias, gnorm_w, state)


def _merge_kernel(oa_ref, ob_ref, wa_ref, wb_ref, ga_ref, gb_ref, o_ref):
    ya = _dot(oa_ref[...], wa_ref[...])
    yb = _dot(ob_ref[...], wb_ref[...])
    o_ref[...] = (_sigmoid(ga_ref[...]) * ya + _sigmoid(gb_ref[...]) * yb).astype(o_ref.dtype)


def _merge(oa, ob, wa, wb, gates, *, tm=512, tn=512):
    M = oa.shape[0]
    D = wa.shape[1]
    tm = min(tm, M)
    tn = min(tn, D)
    nj = D // tn
    return pl.pallas_call(
        _merge_kernel,
        grid=(nj, M // tm),
        in_specs=[pl.BlockSpec((tm, oa.shape[1]), lambda j, i: (i, 0)),
                  pl.BlockSpec((tm, ob.shape[1]), lambda j, i: (i, 0)),
                  pl.BlockSpec((wa.shape[0], tn), lambda j, i: (0, j)),
                  pl.BlockSpec((wb.shape[0], tn), lambda j, i: (0, j)),
                  pl.BlockSpec((tm, tn), lambda j, i: (i, j)),
                  pl.BlockSpec((tm, tn), lambda j, i: (i, j + nj))],
        out_specs=pl.BlockSpec((tm, tn), lambda j, i: (i, j)),
        out_shape=jax.ShapeDtypeStruct((M, D), BF16),
        compiler_params=_params("parallel", "parallel"),
        name="merge",
    )(oa, ob, wa, wb, gates, gates)


def _outproj_kernel(m_ref, w_ref, x_ref, g_ref, b_ref, o_ref, *, alpha):
    mix = _dot(m_ref[...], w_ref[...])
    o_ref[...] = _layer_norm(alpha * x_ref[...] + mix, g_ref[...], b_ref[...])


def _outproj_ln(merged, w, x, g, b, alpha, *, tm=256):
    M, D = x.shape
    tm = min(tm, M)
    row = pl.BlockSpec((1, D), lambda i: (0, 0))
    return pl.pallas_call(
        functools.partial(_outproj_kernel, alpha=alpha),
        grid=(M // tm,),
        in_specs=[pl.BlockSpec((tm, D), lambda i: (i, 0)),
                  pl.BlockSpec((D, D), lambda i: (0, 0)),
                  pl.BlockSpec((tm, D), lambda i: (i, 0)), row, row],
        out_specs=pl.BlockSpec((tm, D), lambda i: (i, 0)),
        out_shape=jax.ShapeDtypeStruct((M, D), F32),
        compiler_params=_params("parallel"),
        name="outproj_ln",
    )(merged, w, x, g, b)


def _ffn_kernel(h_ref, wu_ref, wd_ref, g_ref, b_ref, o_ref, hb_sc, acc_sc, *, alpha):
    f = pl.program_id(1)

    @pl.when(f == 0)
    def _():
        hb_sc[...] = h_ref[...].astype(BF16)
        acc_sc[...] = jnp.zeros(acc_sc.shape, F32)

    u = jnp.maximum(_dot(hb_sc[...], wu_ref[...]), 0.0)
    acc_sc[...] += _dot((u * u).astype(BF16), wd_ref[...])

    @pl.when(f == pl.num_programs(1) - 1)
    def _():
        o_ref[...] = _layer_norm(alpha * h_ref[...] + acc_sc[...], g_ref[...], b_ref[...])


def _ffn_ln(h, wu, wd, g, b, alpha, *, tm=512, tf=512):
    M, D = h.shape
    F = wu.shape[1]
    tm = min(tm, M)
    tf = min(tf, F)
    row = pl.BlockSpec((1, D), lambda i, f: (0, 0))
    return pl.pallas_call(
        functools.partial(_ffn_kernel, alpha=alpha),
        grid=(M // tm, F // tf),
        in_specs=[pl.BlockSpec((tm, D), lambda i, f: (i, 0)),
                  pl.BlockSpec((D, tf), lambda i, f: (0, f)),
                  pl.BlockSpec((tf, D), lambda i, f: (f, 0)), row, row],
        out_specs=pl.BlockSpec((tm, D), lambda i, f: (i, 0)),
        out_shape=jax.ShapeDtypeStruct((M, D), F32),
        scratch_shapes=[pltpu.VMEM((tm, D), BF16), pltpu.VMEM((tm, D), F32)],
        compiler_params=_params("parallel", "arbitrary"),
        name="ffn_ln",
    )(h, wu, wd, g, b)


def _stack(parts):
    return parts[0][None] if len(parts) == 1 else jnp.stack(parts)


def _lambda_init(layer):
    return 0.8 - 0.6 * math.exp(-0.3 * layer)


def _in_projection(x, wts):
    xb = x.astype(BF16)
    (q,) = _matmul(xb, wts["w_q"], [BF16], scale=DA_HEAD_DIM ** -0.5 * math.log2(math.e))
    kv, kv_b = _matmul(xb, wts["w_kv"], [F32, BF16])
    (xc,) = _matmul(xb, wts["w_gd"], [F32])
    (z,) = _matmul(xb, wts["w_z"], [F32])
    (ab,) = _matmul(xb, wts["w_ab"], [F32])
    (gates,) = _matmul(xb, wts["w_gates"], [F32])
    return q, kv, kv_b, xc, z, ab, gates


def _tail(x, oa, ob, gates, wts, alpha):
    merged = _merge(oa, ob, wts["w_proj_a"], wts["w_proj_b"], gates)
    h = _outproj_ln(merged, wts["w_out"], x, wts["ln1_g"], wts["ln1_b"], alpha)
    return _ffn_ln(h, wts["w_up"], wts["w_down"], wts["ln2_g"], wts["ln2_b"], alpha)


def _layer_prompt(x, wts, lam_init, alpha):
    L = x.shape[0]
    nkv = DA_KV_HEADS * 2 * DA_HEAD_DIM
    nk = GD_K_HEADS * GD_DIM
    q, kv, kv_b, xc, z, ab, gates = _in_projection(x, wts)
    oa = _attn_prompt(q, kv_b, wts["subln_w"], wts["lams"], lam_init)
    qn = _conv_silu(xc, wts["conv_w"], 0, nk, normalize=True, scale=GD_DIM ** -0.5)
    kn = _conv_silu(xc, wts["conv_w"], nk, nk, normalize=True)
    vv = _conv_silu(xc, wts["conv_w"], 2 * nk, GD_V_HEADS * GD_DIM, normalize=False)
    ob, state = _gdn_prompt(qn, kn, vv, z, ab[:, :GD_V_HEADS], ab[:, GD_V_HEADS:],
                            wts["a_log"], wts["dt_bias"], wts["gnorm_w"])
    y = _tail(x, oa, ob, gates, wts, alpha)
    return y, kv[:, :nkv], kv[:, nkv:], state, xc[L - (CONV_W - 1):]


def _layer_sample(x, cache_k, cache_v, page_table, state_ssm, state_conv, wts, lam_init, alpha):
    B, T, D = x.shape
    M = B * T
    H = GD_V_HEADS
    nkv = DA_KV_HEADS * 2 * DA_HEAD_DIM
    nk = GD_K_HEADS * GD_DIM
    slabs = cache_k.shape[1]
    x2 = x.reshape(M, D)
    q, kv, _, xc, z, ab, gates = _in_projection(x2, wts)
    qh = q.reshape(B, T, DA_KV_HEADS, DA_REP, 2, DA_HEAD_DIM).transpose(0, 2, 4, 3, 1, 5)
    qh = qh.reshape(B, 2 * DA_KV_HEADS, DA_REP * T, DA_HEAD_DIM)
    qh = jnp.pad(qh, ((0, 0), (0, 0), (0, Q_ROWS - DA_REP * T), (0, 0)))
    k_new = kv[:, :nkv].reshape(B, T * PAGE_SLABS, DA_HEAD_DIM)
    v_new = kv[:, nkv:].reshape(B, T, DA_KV_HEADS, 2, DA_HEAD_DIM).transpose(0, 1, 3, 2, 4)
    v_new = v_new.reshape(B, T * PAGE_SLABS, DA_HEAD_DIM)
    k_new = jnp.pad(k_new, ((0, 0), (0, slabs - T * PAGE_SLABS), (0, 0)))
    v_new = jnp.pad(v_new, ((0, 0), (0, slabs - T * PAGE_SLABS), (0, 0)))
    oa = _attn_sample(qh, cache_k, cache_v, page_table, k_new, v_new, wts["subln_w"], wts["lams"],
                      lam_init, T)
    oa = oa[:, :, :DA_REP * T].reshape(B, DA_KV_HEADS, DA_REP, T, 2 * DA_HEAD_DIM)
    oa = oa.transpose(0, 3, 1, 2, 4).reshape(M, DA_HEADS * 2 * DA_HEAD_DIM).astype(BF16)
    cdim = xc.shape[1]
    xc3 = xc.reshape(B, T, cdim)
    grp = jnp.concatenate([jnp.zeros((B, HALO - (CONV_W - 1) - T, cdim), F32), state_conv, xc3], axis=1)
    grp = grp.reshape(B * HALO, cdim)
    take = lambda y: y.reshape(B, HALO, -1)[:, HALO - T:]
    qn = take(_conv_silu(grp, wts["conv_w"], 0, nk, normalize=True, scale=GD_DIM ** -0.5))
    kn = take(_conv_silu(grp, wts["conv_w"], nk, nk, normalize=True))
    vv = take(_conv_silu(grp, wts["conv_w"], 2 * nk, H * GD_DIM, normalize=False))
    to_cols = lambda y: y.reshape(B, T, GD_K_HEADS, GD_DIM).transpose(0, 2, 3, 1)
    to_heads = lambda y: y.reshape(B, T, H, GD_DIM).transpose(0, 2, 1, 3)
    lanes = lambda y: jnp.broadcast_to(y.reshape(B, T, H).transpose(0, 2, 1)[..., None], (B, H, T, GD_DIM))
    hpar = lambda p: jnp.broadcast_to(p.reshape(H, 1, 1), (H, 1, GD_DIM))
    ob, state = _gdn_sample(to_cols(qn), to_cols(kn), to_heads(vv), to_heads(z),
                            lanes(ab[:, :H]), lanes(ab[:, H:]), hpar(wts["a_log"]),
                            hpar(wts["dt_bias"]), wts["gnorm_w"], state_ssm)
    ob = ob.transpose(0, 2, 1, 3).reshape(M, H * GD_DIM).astype(BF16)
    y = _tail(x2, oa, ob, gates, wts, alpha)
    return (y.reshape(B, T, D), kv[:, :nkv].reshape(B, T, nkv), kv[:, nkv:].reshape(B, T, nkv),
            state, xc3[:, T - (CONV_W - 1):])


def _layer_weights(l, w_in, lambda_q1, lambda_k1, lambda_q2, lambda_k2, subln_w, conv_w, a_log,
                   dt_bias, gnorm_w, w_proj_a, w_proj_b, w_out, ln1_g, ln1_b, w_up, w_down,
                   ln2_g, ln2_b):
    nq = DA_HEADS * 2 * DA_HEAD_DIM
    nkv = DA_KV_HEADS * 2 * DA_HEAD_DIM
    nk = GD_K_HEADS * GD_DIM
    nv = GD_V_HEADS * GD_DIM
    b0 = nq
    b1 = b0 + 2 * nkv
    b2 = b1 + 2 * nk + nv
    b3 = b2 + nv
    b4 = b3 + 2 * GD_V_HEADS
    w = w_in[l]
    row = lambda p: p[l].reshape(1, -1)
    return {
        "w_q": w[:, :b0].astype(BF16), "w_kv": w[:, b0:b1].astype(BF16),
        "w_gd": w[:, b1:b2].astype(BF16), "w_z": w[:, b2:b3].astype(BF16),
        "w_ab": w[:, b3:b4].astype(BF16), "w_gates": w[:, b4:].astype(BF16),
        "lams": (row(lambda_q1), row(lambda_k1), row(lambda_q2), row(lambda_k2)),
        "subln_w": row(subln_w), "conv_w": conv_w[l], "a_log": a_log[l], "dt_bias": dt_bias[l],
        "gnorm_w": row(gnorm_w),
        "w_proj_a": w_proj_a[l].astype(BF16), "w_proj_b": w_proj_b[l].astype(BF16),
        "w_out": w_out[l].astype(BF16), "ln1_g": row(ln1_g), "ln1_b": row(ln1_b),
        "w_up": w_up[l].astype(BF16), "w_down": w_down[l].astype(BF16),
        "ln2_g": row(ln2_g), "ln2_b": row(ln2_b),
    }


def kernel(x_prompt, x_sample, cache_k, cache_v, page_table, state_ssm, state_conv, w_in, lambda_q1, lambda_k1, lambda_q2, lambda_k2, subln_w, conv_w, a_log, dt_bias, gnorm_w, w_proj_a, w_proj_b, w_out, ln1_g, ln1_b, w_up, w_down, ln2_g, ln2_b):
    depth = w_in.shape[0]
    alpha = (2.0 * depth) ** 0.25
    bp, seq, d = x_prompt.shape
    n_pool, page = cache_k.shape[1], cache_k.shape[2]
    slab_view = (depth * n_pool, page * PAGE_SLABS, DA_HEAD_DIM)
    ck = cache_k.reshape(slab_view)
    cv = cache_v.reshape(depth, n_pool, page, DA_KV_HEADS, 2, DA_HEAD_DIM)
    cv = cv.transpose(0, 1, 2, 4, 3, 5).reshape(slab_view)
    yp, ys = x_prompt, x_sample
    outs = [[] for _ in range(8)]
    for l in range(depth):
        wts = _layer_weights(l, w_in, lambda_q1, lambda_k1, lambda_q2, lambda_k2, subln_w, conv_w,
                             a_log, dt_bias, gnorm_w, w_proj_a, w_proj_b, w_out, ln1_g, ln1_b,
                             w_up, w_down, ln2_g, ln2_b)
        lam_init = _lambda_init(l)
        per_seq = [_layer_prompt(yp[i], wts, lam_init, alpha) for i in range(bp)]
        yp, kp, vp, sp, cp = (_stack(t) for t in zip(*per_seq))
        ys, k_s, v_s, s_s, c_s = _layer_sample(
            ys, ck, cv, page_table + l * n_pool, state_ssm[l], state_conv[l], wts, lam_init, alpha)
        kv_shape = (DA_KV_HEADS, 2, DA_HEAD_DIM)
        for lst, val in zip(outs, (kp.reshape(kp.shape[:2] + kv_shape), vp.reshape(vp.shape[:2] + (DA_KV_HEADS, -1)),
                                   sp, cp, k_s.reshape(k_s.shape[:2] + kv_shape),
                                   v_s.reshape(v_s.shape[:2] + (DA_KV_HEADS, -1)), s_s, c_s)):
            lst.append(val)
    return (yp, ys) + tuple(_stack(o) for o in outs)
```

```python
import functools
import math

import jax
import jax.numpy as jnp
from jax import lax
from jax.experimental import pallas as pl
from jax.experimental.pallas import tpu as pltpu

F32 = jnp.float32
BF16 = jnp.bfloat16

DA_HEADS = 8
DA_KV_HEADS = 4
DA_HEAD_DIM = 128
DA_REP = DA_HEADS // DA_KV_HEADS
GD_K_HEADS = 16
GD_V_HEADS = 32
GD_DIM = 128
GD_REP = GD_V_HEADS // GD_K_HEADS
CONV_W = 4
CHUNK = 64
NORM_EPS = 1e-5
L2_EPS = 1e-6
NEG_INF = -1e30

V7X_VMEM_BYTES = 64 * 1024 * 1024
VMEM_LIMIT = V7X_VMEM_BYTES - 12 * 1024 * 1024

_NT = (((1,), (1,)), ((), ()))
_NN = (((1,), (0,)), ((), ()))


def _params(*sem):
    return pltpu.CompilerParams(dimension_semantics=sem, vmem_limit_bytes=VMEM_LIMIT)


def _dot(a, b, dn=_NN):
    return lax.dot_general(a, b, dn, preferred_element_type=F32)


def _split(a):
    hi = a.astype(BF16)
    lo = (a - hi.astype(F32)).astype(BF16)
    return hi, lo


def _dot3(a, b, dn=_NN):
    ah, al = _split(a)
    bh, bl = _split(b)
    return _dot(ah, bh, dn) + (_dot(al, bh, dn) + _dot(ah, bl, dn))


def _softplus(x):
    return jnp.maximum(x, 0.0) + jnp.log(1.0 + jnp.exp(-jnp.abs(x)))


def _sigmoid(x):
    return 1.0 / (1.0 + jnp.exp(-x))


def _lambda(lq1, lk1, lq2, lk2, lam_init):
    a = jnp.sum(lq1[...] * lk1[...], axis=-1, keepdims=True)
    b = jnp.sum(lq2[...] * lk2[...], axis=-1, keepdims=True)
    return jnp.exp(a) - jnp.exp(b) + lam_init


def _layer_norm(x, g, b):
    mu = jnp.mean(x, axis=-1, keepdims=True)
    xc = x - mu
    var = jnp.mean(xc * xc, axis=-1, keepdims=True)
    return xc * lax.rsqrt(var + NORM_EPS) * g + b


HALO = 8


class _Weights:
    def __init__(self, w, ncols, layer=None, col0=0):
        self.w, self.ncols, self.layer, self.col0 = w, ncols, layer, col0
        self.cast = w.dtype != BF16

    def spec(self, K, tn):
        if self.w.ndim == 2:
            assert self.col0 == 0
            return pl.BlockSpec((K, tn), lambda j, i: (0, j))
        off, layer = self.col0 // tn, self.layer
        assert self.col0 % tn == 0
        return pl.BlockSpec((None, K, tn), lambda j, i: (layer, 0, j + off),
                            pipeline_mode=pl.Buffered(1))

    def scratch(self, K, tn):
        return [pltpu.VMEM((K, tn), BF16)] if self.cast else []


def _weight_tile(w_ref, scr, cast):
    if not cast:
        return w_ref[...]
    wb_sc = scr[0]

    @pl.when(pl.program_id(1) == 0)
    def _():
        wb_sc[...] = w_ref[...].astype(BF16)

    return wb_sc[...]


def _proj_kernel(x_ref, w_ref, *rest, n_out, cast, scale, act):
    acc = _dot(x_ref[...], _weight_tile(w_ref, rest[n_out:], cast))
    if scale != 1.0:
        acc = acc * scale
    if act == "silu":
        acc = acc * _sigmoid(acc)
    elif act == "sigmoid":
        acc = _sigmoid(acc)
    for o in rest[:n_out]:
        o[...] = acc.astype(o.dtype)


def _project(x, wt, out_dtypes, *, scale=1.0, act=None, tm=1024, tn=1024):
    M, K = x.shape
    N = wt.ncols
    tm = min(tm, M)
    tn = min(tn, N)
    return pl.pallas_call(
        functools.partial(_proj_kernel, n_out=len(out_dtypes), cast=wt.cast, scale=scale, act=act),
        grid=(N // tn, M // tm),
        in_specs=[pl.BlockSpec((tm, K), lambda j, i: (i, 0)), wt.spec(K, tn)],
        out_specs=[pl.BlockSpec((tm, tn), lambda j, i: (i, j)) for _ in out_dtypes],
        out_shape=[jax.ShapeDtypeStruct((M, N), d) for d in out_dtypes],
        scratch_shapes=wt.scratch(K, tn),
        compiler_params=_params("parallel", "arbitrary"),
        name="in_proj",
    )(x, wt.w)


def _proj_slab_kernel(x_ref, w_ref, o3_ref, ob_ref, *scr, cast, order):
    acc = _dot(x_ref[...], _weight_tile(w_ref, scr, cast))
    ob_ref[...] = acc.astype(ob_ref.dtype)
    for slab, cb in enumerate(order):
        o3_ref[:, slab, :] = acc[:, cb * DA_HEAD_DIM:(cb + 1) * DA_HEAD_DIM]


def _project_slabs(x, wt, order, *, tm=1024):
    M, K = x.shape
    N = wt.ncols
    tm = min(tm, M)
    return pl.pallas_call(
        functools.partial(_proj_slab_kernel, cast=wt.cast, order=order),
        grid=(1, M // tm),
        in_specs=[pl.BlockSpec((tm, K), lambda j, i: (i, 0)), wt.spec(K, N)],
        out_specs=[pl.BlockSpec((tm, len(order), DA_HEAD_DIM), lambda j, i: (i, 0, 0)),
                   pl.BlockSpec((tm, N), lambda j, i: (i, 0))],
        out_shape=[jax.ShapeDtypeStruct((M, len(order), DA_HEAD_DIM), F32),
                   jax.ShapeDtypeStruct((M, N), BF16)],
        scratch_shapes=wt.scratch(K, N),
        compiler_params=_params("parallel", "arbitrary"),
        name="kv_proj",
    )(x, wt.w)


def _conv_silu_rows(xs_sc, cw_ref, o_ref, *, tm, tc, normalize, scale):
    y = cw_ref[CONV_W - 1:CONV_W, :] * xs_sc[HALO:HALO + tm, :]
    for back in range(1, CONV_W):
        y = y + cw_ref[CONV_W - 1 - back:CONV_W - back, :] * xs_sc[HALO - back:HALO - back + tm, :]
    y = y * _sigmoid(y)
    if normalize:
        for h in range(tc // GD_DIM):
            ys = y[:, h * GD_DIM:(h + 1) * GD_DIM]
            ss = jnp.sum(ys * ys, axis=-1, keepdims=True)
            o_ref[:, h * GD_DIM:(h + 1) * GD_DIM] = ys * (lax.rsqrt(ss + L2_EPS) * scale)
    else:
        o_ref[...] = y
    xs_sc[0:HALO, :] = xs_sc[tm:tm + HALO, :]


def _proj_conv_kernel(x_ref, w_ref, cw_ref, o_ref, tail_ref, *scr, tm, tn, cast, normalize, scale):
    xs_sc = scr[-1]

    @pl.when(pl.program_id(1) == 0)
    def _():
        xs_sc[0:HALO, :] = jnp.zeros((HALO, tn), F32)

    xs_sc[HALO:HALO + tm, :] = _dot(x_ref[...], _weight_tile(w_ref, scr, cast))
    tail_ref[...] = xs_sc[tm:tm + HALO, :]
    _conv_silu_rows(xs_sc, cw_ref, o_ref, tm=tm, tc=tn, normalize=normalize, scale=scale)


def _project_conv(x, wt, conv_w, conv_col0, *, normalize, scale=1.0, tm=1024, tn=1024):
    M, K = x.shape
    N = wt.ncols
    tm = min(tm, M)
    tn = min(tn, N)
    coff = conv_col0 // tn
    assert conv_col0 % tn == 0
    return pl.pallas_call(
        functools.partial(_proj_conv_kernel, tm=tm, tn=tn, cast=wt.cast, normalize=normalize,
                          scale=scale),
        grid=(N // tn, M // tm),
        in_specs=[pl.BlockSpec((tm, K), lambda j, i: (i, 0)), wt.spec(K, tn),
                  pl.BlockSpec((CONV_W, tn), lambda j, i: (0, j + coff))],
        out_specs=[pl.BlockSpec((tm, tn), lambda j, i: (i, j)),
                   pl.BlockSpec((HALO, tn), lambda j, i: (0, j))],
        out_shape=[jax.ShapeDtypeStruct((M, N), F32), jax.ShapeDtypeStruct((HALO, N), F32)],
        scratch_shapes=wt.scratch(K, tn) + [pltpu.VMEM((tm + HALO, tn), F32)],
        compiler_params=_params("parallel", "arbitrary"),
        name="gd_proj_conv",
    )(x, wt.w, conv_w)


def _subln_out(acc0, l0, acc1, l1, lam, subw, lam_init):
    o = acc0 / l0 - lam * (acc1 / l1)
    o = o * lax.rsqrt(jnp.mean(o * o, axis=-1, keepdims=True) + NORM_EPS)
    return o * subw * (1.0 - lam_init)


def _attn_prompt_kernel(q_ref, k_ref, v_ref, subw_ref, lq1, lk1, lq2, lk2, o_ref,
                        m_sc, l_sc, acc_sc, *, tq, tk, lam_init):
    qi = pl.program_id(1)
    ki = pl.program_id(2)
    last = (qi * tq + tq - 1) // tk

    @pl.when(ki == 0)
    def _():
        m_sc[...] = jnp.full(m_sc.shape, -jnp.inf, F32)
        l_sc[...] = jnp.zeros(l_sc.shape, F32)
        acc_sc[...] = jnp.zeros(acc_sc.shape, F32)

    def step(masked):
        v = v_ref[...]
        if masked:
            ahead = (lax.broadcasted_iota(jnp.int32, (tq, tk), 1)
                     - lax.broadcasted_iota(jnp.int32, (tq, tk), 0))
            keep = ahead <= qi * tq - ki * tk
        for r in range(DA_REP):
            for c in range(2):
                idx = r * 2 + c
                q = q_ref[:, idx * DA_HEAD_DIM:(idx + 1) * DA_HEAD_DIM]
                k = k_ref[:, c * DA_HEAD_DIM:(c + 1) * DA_HEAD_DIM]
                s = _dot(q, k, _NT)
                if masked:
                    s = jnp.where(keep, s, NEG_INF)
                m_prev = m_sc[idx]
                m_new = jnp.maximum(m_prev, jnp.max(s, axis=-1, keepdims=True))
                alpha = jnp.exp2(m_prev - m_new)
                p = jnp.exp2(s - m_new)
                l_sc[idx] = alpha * l_sc[idx] + jnp.sum(p, axis=-1, keepdims=True)
                acc_sc[idx] = alpha * acc_sc[idx] + _dot(p.astype(BF16), v)
                m_sc[idx] = m_new

    @pl.when(ki < last)
    def _():
        step(False)

    @pl.when(ki == last)
    def _():
        step(True)
        lam = _lambda(lq1, lk1, lq2, lk2, lam_init)
        e = 2 * DA_HEAD_DIM
        for r in range(DA_REP):
            o = _subln_out(acc_sc[r * 2], l_sc[r * 2], acc_sc[r * 2 + 1], l_sc[r * 2 + 1],
                           lam, subw_ref[...], lam_init)
            o_ref[:, r * e:(r + 1) * e] = o.astype(o_ref.dtype)


def _attn_prompt(q, k, v, subw, lams, lam_init, *, tq=512, tk=2048):
    L = q.shape[0]
    tq = min(tq, L)
    tk = min(tk, L)
    assert tk % tq == 0
    e = 2 * DA_HEAD_DIM
    vec = pl.BlockSpec((1, DA_HEAD_DIM), lambda g, i, j: (0, 0))
    kblk = lambda i, j: jnp.minimum(j, (i * tq + tq - 1) // tk)
    return pl.pallas_call(
        functools.partial(_attn_prompt_kernel, tq=tq, tk=tk, lam_init=lam_init),
        grid=(DA_KV_HEADS, L // tq, L // tk),
        in_specs=[pl.BlockSpec((tq, DA_REP * e), lambda g, i, j: (i, g)),
                  pl.BlockSpec((tk, e), lambda g, i, j: (kblk(i, j), g)),
                  pl.BlockSpec((tk, e), lambda g, i, j: (kblk(i, j), g)),
                  pl.BlockSpec((1, e), lambda g, i, j: (0, 0)),
                  vec, vec, vec, vec],
        out_specs=pl.BlockSpec((tq, DA_REP * e), lambda g, i, j: (i, g)),
        out_shape=jax.ShapeDtypeStruct((L, DA_HEADS * e), BF16),
        scratch_shapes=[pltpu.VMEM((2 * DA_REP, tq, 1), F32),
                        pltpu.VMEM((2 * DA_REP, tq, 1), F32),
                        pltpu.VMEM((2 * DA_REP, tq, e), F32)],
        compiler_params=_params("parallel", "parallel", "arbitrary"),
        name="attn_prompt",
    )(q, k, v, subw, *lams)


Q_ROWS = 16


PAGE_SLABS = 2 * DA_KV_HEADS


def _attn_sample_kernel(pt_ref, q_ref, *rest, pages, page, dec_seq, lam_init):
    k_refs = rest[:pages]
    v_refs = rest[pages:2 * pages]
    kn_ref, vn_ref, subw_ref, lq1, lk1, lq2, lk2, o_ref, m_sc, l_sc, acc_sc = rest[2 * pages:]
    j = pl.program_id(1)
    rows = 2 * DA_KV_HEADS * Q_ROWS

    @pl.when(j == 0)
    def _():
        m_sc[...] = jnp.full(m_sc.shape, -jnp.inf, F32)
        l_sc[...] = jnp.zeros(l_sc.shape, F32)
        acc_sc[...] = jnp.zeros(acc_sc.shape, F32)

    def slab(ref, i):
        return ref[0, pl.ds(i, page, stride=PAGE_SLABS), :]

    def update(k_list, v_list, keep):
        s = []
        for idx in range(2 * DA_KV_HEADS):
            k = jnp.concatenate([slab(kr, idx) for kr in k_list], axis=0).astype(BF16)
            s.append(_dot(q_ref[0, idx], k, _NT))
        s = jnp.concatenate(s, axis=0)
        if keep is not None:
            s = jnp.where(keep, s, NEG_INF)
        m_prev = m_sc[...]
        m_new = jnp.maximum(m_prev, jnp.max(s, axis=-1, keepdims=True))
        alpha = jnp.exp2(m_prev - m_new)
        p = jnp.exp2(s - m_new)
        l_sc[...] = alpha * l_sc[...] + jnp.sum(p, axis=-1, keepdims=True)
        m_sc[...] = m_new
        p = p.astype(BF16)
        pv = []
        for g in range(DA_KV_HEADS):
            v = jnp.concatenate(
                [jnp.concatenate([slab(vr, g), slab(vr, DA_KV_HEADS + g)], axis=1) for vr in v_list],
                axis=0).astype(BF16)
            pv.append(_dot(p[g * 2 * Q_ROWS:(g + 1) * 2 * Q_ROWS], v))
        acc_sc[...] = alpha * acc_sc[...] + jnp.concatenate(pv, axis=0)

    update(k_refs, v_refs, None)

    @pl.when(j == pl.num_programs(1) - 1)
    def _():
        row = lax.broadcasted_iota(jnp.int32, (rows, page), 0)
        col = lax.broadcasted_iota(jnp.int32, (rows, page), 1)
        update([kn_ref], [vn_ref], col <= (row % Q_ROWS) % dec_seq)
        lam = _lambda(lq1, lk1, lq2, lk2, lam_init)
        for g in range(DA_KV_HEADS):
            r0 = slice(2 * g * Q_ROWS, (2 * g + 1) * Q_ROWS)
            r1 = slice((2 * g + 1) * Q_ROWS, (2 * g + 2) * Q_ROWS)
            o = _subln_out(acc_sc[r0, :], l_sc[r0, :], acc_sc[r1, :], l_sc[r1, :],
                           lam, subw_ref[...], lam_init)
            o_ref[0, g] = o.astype(o_ref.dtype)


def _attn_sample(q, cache_k, cache_v, page_table, k_new, v_new, subw, lams, lam_init, dec_seq):
    B = q.shape[0]
    n_pages = page_table.shape[1]
    page = cache_k.shape[1] // PAGE_SLABS
    pages = math.gcd(n_pages, 8)
    e = 2 * DA_HEAD_DIM
    rows = 2 * DA_KV_HEADS * Q_ROWS

    def page_spec(i):
        return pl.BlockSpec((1, page * PAGE_SLABS, DA_HEAD_DIM),
                            lambda b, j, pt: (pt[b, j * pages + i], 0, 0))

    vec = pl.BlockSpec((1, DA_HEAD_DIM), lambda b, j, pt: (0, 0))
    new_spec = pl.BlockSpec((1, page * PAGE_SLABS, DA_HEAD_DIM), lambda b, j, pt: (b, 0, 0))
    grid_spec = pltpu.PrefetchScalarGridSpec(
        num_scalar_prefetch=1,
        grid=(B, n_pages // pages),
        in_specs=([pl.BlockSpec((1, 2 * DA_KV_HEADS, Q_ROWS, DA_HEAD_DIM), lambda b, j, pt: (b, 0, 0, 0))]
                  + [page_spec(i) for i in range(pages)] + [page_spec(i) for i in range(pages)]
                  + [new_spec, new_spec, pl.BlockSpec((1, e), lambda b, j, pt: (0, 0)),
                     vec, vec, vec, vec]),
        out_specs=pl.BlockSpec((1, DA_KV_HEADS, Q_ROWS, e), lambda b, j, pt: (b, 0, 0, 0)),
        scratch_shapes=[pltpu.VMEM((rows, 1), F32),
                        pltpu.VMEM((rows, 1), F32),
                        pltpu.VMEM((rows, e), F32)])
    return pl.pallas_call(
        functools.partial(_attn_sample_kernel, pages=pages, page=page, dec_seq=dec_seq,
                          lam_init=lam_init),
        grid_spec=grid_spec,
        out_shape=jax.ShapeDtypeStruct((B, DA_KV_HEADS, Q_ROWS, e), F32),
        compiler_params=_params("parallel", "arbitrary"),
        name="attn_sample",
    )(page_table, q, *([cache_k] * pages), *([cache_v] * pages), k_new, v_new, subw, *lams)


def _conv_kernel(x_ref, w_ref, o_ref, xs_sc, *, tm, tc, normalize, scale):
    @pl.when(pl.program_id(1) == 0)
    def _():
        xs_sc[0:HALO, :] = jnp.zeros((HALO, tc), F32)

    xs_sc[HALO:HALO + tm, :] = x_ref[...]
    _conv_silu_rows(xs_sc, w_ref, o_ref, tm=tm, tc=tc, normalize=normalize, scale=scale)


def _conv_silu(x, w, col0, ncols, *, normalize, scale=1.0, tm=512, tc=512):
    L = x.shape[0]
    tm = min(tm, L)
    off = col0 // tc
    return pl.pallas_call(
        functools.partial(_conv_kernel, tm=tm, tc=tc, normalize=normalize, scale=scale),
        grid=(ncols // tc, L // tm),
        in_specs=[pl.BlockSpec((tm, tc), lambda c, i: (i, c + off)),
                  pl.BlockSpec((CONV_W, tc), lambda c, i: (0, c + off))],
        out_specs=pl.BlockSpec((tm, tc), lambda c, i: (i, c)),
        out_shape=jax.ShapeDtypeStruct((L, ncols), F32),
        scratch_shapes=[pltpu.VMEM((tm + HALO, tc), F32)],
        compiler_params=_params("parallel", "arbitrary"),
        name="conv_silu",
    )(x, w)


GROUP_CHUNKS = 4
SCAN_HEADS = 4


def _gdn_intra_kernel(q_ref, k_ref, v_ref, arow_ref, brow_ref, alog_ref, dtb_ref,
                      mq_ref, bo_ref, egl_ref, *, groups, C, cpg):
    gt = cpg * C
    shift = C.bit_length() - 1
    row = lax.broadcasted_iota(jnp.int32, (gt, gt), 0)
    col = lax.broadcasted_iota(jnp.int32, (gt, gt), 1)
    same = (row >> shift) == (col >> shift)
    tril = same & (col <= row)
    strict = same & (col < row)
    diag = row == col
    eye = jnp.where(diag, 1.0, 0.0)
    r2 = lax.broadcasted_iota(jnp.int32, (cpg * GD_DIM, gt), 0)
    c2 = lax.broadcasted_iota(jnp.int32, (cpg * GD_DIM, gt), 1)
    ke_mask = (r2 >> (GD_DIM.bit_length() - 1)) == (c2 >> shift)
    units = []
    for gi in range(groups):
        rows = slice(gi * gt, (gi + 1) * gt)
        q = q_ref[rows, :]
        k = k_ref[rows, :]
        k16 = k.astype(BF16)
        raw = _dot(jnp.concatenate([q.astype(BF16), k16], axis=0), k16, _NT)
        for j in range(GD_REP):
            neg_a = -jnp.exp(alog_ref[j])
            dtb = dtb_ref[j]
            g_row = neg_a * _softplus(arow_ref[j, gi] + dtb)
            beta = jnp.sum(jnp.where(diag, _sigmoid(brow_ref[j, gi]), 0.0), axis=1, keepdims=True)
            gc_col = jnp.sum(jnp.where(tril, g_row, 0.0), axis=1, keepdims=True)
            gc_row = jnp.sum(jnp.where(diag, gc_col, 0.0), axis=0, keepdims=True)
            gl_col = jnp.sum(jnp.where(same, g_row, 0.0), axis=1, keepdims=True)
            decay = jnp.where(tril, jnp.exp(jnp.where(tril, gc_col - gc_row, 0.0)), 0.0)
            lm = jnp.where(strict, beta * raw[gt:] * decay, 0.0)
            units.append(dict(gi=gi, j=j, rows=rows, beta=beta, gc_col=gc_col, gl_col=gl_col,
                              qk=raw[:gt] * decay, x=eye - lm, p16=lm.astype(BF16)))
    n = 1
    while 2 * n < C:
        for u in units:
            u["p16"] = _dot(u["p16"], u["p16"]).astype(BF16)
        for u in units:
            u["x"] = u["x"] + _dot(u["x"].astype(BF16), u["p16"])
        n *= 2
    for u in units:
        k = k_ref[u["rows"], :]
        v = v_ref[u["rows"], u["j"] * GD_DIM:(u["j"] + 1) * GD_DIM]
        u["eg"] = jnp.exp(u["gc_col"])
        u["wu"] = _dot(u["x"].astype(BF16),
                       jnp.concatenate([k * (u["beta"] * u["eg"]), v * u["beta"]], axis=1).astype(BF16))
    for u in units:
        gi, j = u["gi"], u["j"]
        q = q_ref[u["rows"], :]
        k = k_ref[u["rows"], :]
        ket = (k * jnp.exp(u["gl_col"] - u["gc_col"])).T
        ke_bd = jnp.where(ke_mask, jnp.concatenate([ket] * cpg, axis=0), 0.0)
        res = _dot(jnp.concatenate([ke_bd, u["qk"]], axis=0).astype(BF16), u["wu"].astype(BF16))
        qe = q * u["eg"]
        for c in range(cpg):
            n_idx = gi * cpg + c
            top = slice(c * GD_DIM, (c + 1) * GD_DIM)
            bot = slice(cpg * GD_DIM + c * C, cpg * GD_DIM + (c + 1) * C)
            mq_ref[j, n_idx, 0:GD_DIM, :] = res[top, :GD_DIM].astype(BF16)
            mq_ref[j, n_idx, GD_DIM:GD_DIM + C, :] = (
                qe[c * C:(c + 1) * C] - res[bot, :GD_DIM]).astype(BF16)
            bo_ref[j, n_idx, 0:GD_DIM, :] = res[top, GD_DIM:]
            bo_ref[j, n_idx, GD_DIM:GD_DIM + C, :] = res[bot, GD_DIM:]
            egl_ref[j, n_idx] = jnp.broadcast_to(
                jnp.exp(u["gl_col"][c * C:c * C + 1, :]), (1, GD_DIM))


def _gdn_scan_kernel(mq_ref, bo_ref, egl_ref, zs_ref, gw_ref, o_ref, s_ref, s_sc, *, nb, C, hb):
    n = pl.program_id(1)

    @pl.when(n == 0)
    def _():
        s_sc[...] = jnp.zeros(s_sc.shape, F32)

    s = [s_sc[h] for h in range(hb)]
    for c in range(nb):
        rows = slice(c * C, (c + 1) * C)
        for h in range(hb):
            cols = slice(h * GD_DIM, (h + 1) * GD_DIM)
            r = _dot(mq_ref[h, c], s[h].astype(BF16))
            s[h] = egl_ref[h, c] * s[h] - r[:GD_DIM] + bo_ref[h, c, 0:GD_DIM, :]
            o = r[GD_DIM:] + bo_ref[h, c, GD_DIM:GD_DIM + C, :]
            o = o * lax.rsqrt(jnp.mean(o * o, axis=-1, keepdims=True) + NORM_EPS) * gw_ref[...]
            o_ref[rows, cols] = (o * zs_ref[rows, cols].astype(F32)).astype(o_ref.dtype)
    for h in range(hb):
        s_sc[h] = s[h]

    @pl.when(n == pl.num_programs(1) - 1)
    def _():
        s_ref[...] = s_sc[...]


def _gdn_prompt(qn, kn, vv, zs, a, b, a_log, dt_bias, gnorm_w, *, groups=4):
    L = qn.shape[0]
    H = GD_V_HEADS
    C = min(CHUNK, L)
    assert C & (C - 1) == 0 and L % C == 0
    N = L // C
    cpg = math.gcd(GROUP_CHUNKS, N)
    gt = cpg * C
    ng = N // cpg
    groups = math.gcd(groups, ng)
    nb = groups * cpg
    hb = SCAN_HEADS
    a_t = a.T.reshape(H, ng, gt)
    b_t = b.T.reshape(H, ng, gt)
    hvec = lambda x: x.reshape(H, 1, 1)
    grp = lambda *shape: pl.BlockSpec((GD_REP, groups) + shape, lambda hk, n: (hk, n, 0, 0))
    per_chunk = lambda heads, *shape: pl.BlockSpec((heads, nb) + shape, lambda h, n: (h, n, 0, 0))
    mq, bo, egl = pl.pallas_call(
        functools.partial(_gdn_intra_kernel, groups=groups, C=C, cpg=cpg),
        grid=(GD_K_HEADS, ng // groups),
        in_specs=[pl.BlockSpec((groups * gt, GD_DIM), lambda hk, n: (n, hk)),
                  pl.BlockSpec((groups * gt, GD_DIM), lambda hk, n: (n, hk)),
                  pl.BlockSpec((groups * gt, GD_REP * GD_DIM), lambda hk, n: (n, hk)),
                  grp(1, gt), grp(1, gt),
                  pl.BlockSpec((GD_REP, 1, 1), lambda hk, n: (hk, 0, 0)),
                  pl.BlockSpec((GD_REP, 1, 1), lambda hk, n: (hk, 0, 0))],
        out_specs=[per_chunk(GD_REP, GD_DIM + C, GD_DIM), per_chunk(GD_REP, GD_DIM + C, GD_DIM),
                   per_chunk(GD_REP, 1, GD_DIM)],
        out_shape=[jax.ShapeDtypeStruct((H, N, GD_DIM + C, GD_DIM), BF16),
                   jax.ShapeDtypeStruct((H, N, GD_DIM + C, GD_DIM), F32),
                   jax.ShapeDtypeStruct((H, N, 1, GD_DIM), F32)],
        compiler_params=_params("parallel", "parallel"),
        name="gdn_intra",
    )(qn, kn, vv, a_t[:, :, None, :], b_t[:, :, None, :], hvec(a_log), hvec(dt_bias))
    out, state = pl.pallas_call(
        functools.partial(_gdn_scan_kernel, nb=nb, C=C, hb=hb),
        grid=(H // hb, N // nb),
        in_specs=[per_chunk(hb, GD_DIM + C, GD_DIM), per_chunk(hb, GD_DIM + C, GD_DIM),
                  per_chunk(hb, 1, GD_DIM),
                  pl.BlockSpec((nb * C, hb * GD_DIM), lambda h, n: (n, h)),
                  pl.BlockSpec((1, GD_DIM), lambda h, n: (0, 0))],
        out_specs=[pl.BlockSpec((nb * C, hb * GD_DIM), lambda h, n: (n, h)),
                   pl.BlockSpec((hb, GD_DIM, GD_DIM), lambda h, n: (h, 0, 0))],
        out_shape=[jax.ShapeDtypeStruct((L, H * GD_DIM), BF16),
                   jax.ShapeDtypeStruct((H, GD_DIM, GD_DIM), F32)],
        scratch_shapes=[pltpu.VMEM((hb, GD_DIM, GD_DIM), F32)],
        compiler_params=_params("parallel", "arbitrary"),
        name="gdn_scan",
    )(mq, bo, egl, zs, gnorm_w)
    return out, state


def _gdn_sample_kernel(qt_ref, kt_ref, v_ref, zs_ref, a_ref, b_ref, alog_ref, dtb_ref, gw_ref,
                       s_ref, o_ref, so_ref, *, T):
    def key_head(hk, carry):
        qt = qt_ref[0, hk]
        kt = kt_ref[0, hk]
        heads = [hk * GD_REP + j for j in range(GD_REP)]
        s = [s_ref[0, h] for h in heads]
        g = [-jnp.exp(alog_ref[h]) * _softplus(a_ref[0, h] + dtb_ref[h]) for h in heads]
        beta = [_sigmoid(b_ref[0, h]) for h in heads]
        v = [v_ref[0, h] for h in heads]
        outs = [[] for _ in heads]
        for t in range(T):
            kc = kt[:, t:t + 1]
            qc = qt[:, t:t + 1]
            for j in range(GD_REP):
                sj = s[j] * jnp.exp(g[j][t:t + 1, :])
                v_new = beta[j][t:t + 1, :] * (v[j][t:t + 1, :] - jnp.sum(kc * sj, axis=0, keepdims=True))
                s[j] = sj + kc * v_new
                outs[j].append(jnp.sum(qc * s[j], axis=0, keepdims=True))
        for j, h in enumerate(heads):
            so_ref[0, h] = s[j]
            o = jnp.concatenate(outs[j], axis=0)
            o = o * lax.rsqrt(jnp.mean(o * o, axis=-1, keepdims=True) + NORM_EPS) * gw_ref[...]
            o_ref[0, h] = o * zs_ref[0, h]
        return carry

    lax.fori_loop(0, GD_K_HEADS, key_head, 0)


def _gdn_sample(qt, kt, v, z, a, b, a_log, dt_bias, gnorm_w, state):
    B, H, T, _ = v.shape
    per_b = lambda *shape: pl.BlockSpec((1,) + shape, lambda i: (i,) + (0,) * len(shape))
    hpar = pl.BlockSpec((H, 1, GD_DIM), lambda i: (0, 0, 0))
    return pl.pallas_call(
        functools.partial(_gdn_sample_kernel, T=T),
        grid=(B,),
        in_specs=[per_b(GD_K_HEADS, GD_DIM, T), per_b(GD_K_HEADS, GD_DIM, T),
                  per_b(H, T, GD_DIM), per_b(H, T, GD_DIM), per_b(H, T, GD_DIM), per_b(H, T, GD_DIM),
                  hpar, hpar, pl.BlockSpec((1, GD_DIM), lambda i: (0, 0)),
                  per_b(H, GD_DIM, GD_DIM)],
        out_specs=[per_b(H, T, GD_DIM), per_b(H, GD_DIM, GD_DIM)],
        out_shape=[jax.ShapeDtypeStruct((B, H, T, GD_DIM), F32),
                   jax.ShapeDtypeStruct((B, H, GD_DIM, GD_DIM), F32)],
        compiler_params=_params("parallel"),
        name="gdn_sample",
    )(qt, kt, v, z, a, b, a_log, dt_bias, gnorm_w, state)


def _merge_kernel(oa_ref, ob_ref, wa_ref, wb_ref, ga_ref, gb_ref, o_ref):
    ya = _dot(oa_ref[...], wa_ref[...])
    yb = _dot(ob_ref[...], wb_ref[...])
    o_ref[...] = (ga_ref[...].astype(F32) * ya + gb_ref[...].astype(F32) * yb).astype(o_ref.dtype)


def _merge(oa, ob, wa, wb, gates, *, tm=512, tn=512):
    M = oa.shape[0]
    D = wa.shape[1]
    tm = min(tm, M)
    tn = min(tn, D)
    nj = D // tn
    return pl.pallas_call(
        _merge_kernel,
        grid=(nj, M // tm),
        in_specs=[pl.BlockSpec((tm, oa.shape[1]), lambda j, i: (i, 0)),
                  pl.BlockSpec((tm, ob.shape[1]), lambda j, i: (i, 0)),
                  pl.BlockSpec((wa.shape[0], tn), lambda j, i: (0, j)),
                  pl.BlockSpec((wb.shape[0], tn), lambda j, i: (0, j)),
                  pl.BlockSpec((tm, tn), lambda j, i: (i, j)),
                  pl.BlockSpec((tm, tn), lambda j, i: (i, j + nj))],
        out_specs=pl.BlockSpec((tm, tn), lambda j, i: (i, j)),
        out_shape=jax.ShapeDtypeStruct((M, D), BF16),
        compiler_params=_params("parallel", "parallel"),
        name="merge",
    )(oa, ob, wa, wb, gates, gates)


def _outproj_kernel(m_ref, w_ref, x_ref, g_ref, b_ref, o_ref, *, alpha):
    mix = _dot(m_ref[...], w_ref[...])
    o_ref[...] = _layer_norm(alpha * x_ref[...] + mix, g_ref[...], b_ref[...])


def _outproj_ln(merged, w, x, g, b, alpha, *, tm=256):
    M, D = x.shape
    tm = min(tm, M)
    row = pl.BlockSpec((1, D), lambda i: (0, 0))
    return pl.pallas_call(
        functools.partial(_outproj_kernel, alpha=alpha),
        grid=(M // tm,),
        in_specs=[pl.BlockSpec((tm, D), lambda i: (i, 0)),
                  pl.BlockSpec((D, D), lambda i: (0, 0)),
                  pl.BlockSpec((tm, D), lambda i: (i, 0)), row, row],
        out_specs=pl.BlockSpec((tm, D), lambda i: (i, 0)),
        out_shape=jax.ShapeDtypeStruct((M, D), F32),
        compiler_params=_params("parallel"),
        name="outproj_ln",
    )(merged, w, x, g, b)


def _ffn_kernel(h_ref, wu_ref, wd_ref, g_ref, b_ref, o_ref, hb_sc, acc_sc, *, alpha):
    f = pl.program_id(1)

    @pl.when(f == 0)
    def _():
        hb_sc[...] = h_ref[...].astype(BF16)
        acc_sc[...] = jnp.zeros(acc_sc.shape, F32)

    u = jnp.maximum(_dot(hb_sc[...], wu_ref[...]), 0.0)
    acc_sc[...] += _dot((u * u).astype(BF16), wd_ref[...])

    @pl.when(f == pl.num_programs(1) - 1)
    def _():
        o_ref[...] = _layer_norm(alpha * h_ref[...] + acc_sc[...], g_ref[...], b_ref[...])


def _ffn_ln(h, wu, wd, g, b, alpha, *, tm=512, tf=512):
    M, D = h.shape
    F = wu.shape[1]
    tm = min(tm, M)
    tf = min(tf, F)
    row = pl.BlockSpec((1, D), lambda i, f: (0, 0))
    return pl.pallas_call(
        functools.partial(_ffn_kernel, alpha=alpha),
        grid=(M // tm, F // tf),
        in_specs=[pl.BlockSpec((tm, D), lambda i, f: (i, 0)),
                  pl.BlockSpec((D, tf), lambda i, f: (0, f)),
                  pl.BlockSpec((tf, D), lambda i, f: (f, 0)), row, row],
        out_specs=pl.BlockSpec((tm, D), lambda i, f: (i, 0)),
        out_shape=jax.ShapeDtypeStruct((M, D), F32),
        scratch_shapes=[pltpu.VMEM((tm, D), BF16), pltpu.VMEM((tm, D), F32)],
        compiler_params=_params("parallel", "arbitrary"),
        name="ffn_ln",
    )(h, wu, wd, g, b)


def _stack(parts):
    return parts[0][None] if len(parts) == 1 else jnp.stack(parts)


def _lambda_init(layer):
    return 0.8 - 0.6 * math.exp(-0.3 * layer)


K_SLAB_ORDER = tuple(range(PAGE_SLABS))
V_SLAB_ORDER = tuple(g * 2 + half for half in range(2) for g in range(DA_KV_HEADS))


def _shared_projections(xb, wts):
    (q,) = _project(xb, wts["q"], [BF16], scale=DA_HEAD_DIM ** -0.5 * math.log2(math.e))
    k3, kb = _project_slabs(xb, wts["k"], K_SLAB_ORDER)
    v3, vb = _project_slabs(xb, wts["v"], V_SLAB_ORDER)
    (zs,) = _project(xb, wts["z"], [BF16], act="silu")
    (ab,) = _project(xb, wts["ab"], [F32])
    (gates,) = _project(xb, wts["gates"], [BF16], act="sigmoid")
    return q, k3, kb, v3, vb, zs, ab, gates


def _tail(x, oa, ob, gates, wts, alpha):
    merged = _merge(oa, ob, wts["w_proj_a"], wts["w_proj_b"], gates)
    h = _outproj_ln(merged, wts["w_out"], x, wts["ln1_g"], wts["ln1_b"], alpha)
    return _ffn_ln(h, wts["w_up"], wts["w_down"], wts["ln2_g"], wts["ln2_b"], alpha)


def _layer_prompt(x, wts, lam_init, alpha):
    nk = GD_K_HEADS * GD_DIM
    xb = x.astype(BF16)
    q, k3, kb, v3, vb, zs, ab, gates = _shared_projections(xb, wts)
    oa = _attn_prompt(q, kb, vb, wts["subln_w"], wts["lams"], lam_init)
    qn, tq = _project_conv(xb, wts["gq"], wts["conv_w"], 0, normalize=True, scale=GD_DIM ** -0.5)
    kn, tk = _project_conv(xb, wts["gk"], wts["conv_w"], nk, normalize=True)
    vv, tv = _project_conv(xb, wts["gv"], wts["conv_w"], 2 * nk, normalize=False)
    ob, state = _gdn_prompt(qn, kn, vv, zs, ab[:, :GD_V_HEADS], ab[:, GD_V_HEADS:],
                            wts["a_log"], wts["dt_bias"], wts["gnorm_w"])
    y = _tail(x, oa, ob, gates, wts, alpha)
    conv_rows = jnp.concatenate([tq, tk, tv], axis=1)[HALO - (CONV_W - 1):]
    return y, k3, v3, state, conv_rows


def _layer_sample(x, cache_k, cache_v, page_table, state_ssm, state_conv, wts, lam_init, alpha):
    B, T, D = x.shape
    M = B * T
    H = GD_V_HEADS
    nk = GD_K_HEADS * GD_DIM
    slabs = cache_k.shape[1]
    x2 = x.reshape(M, D)
    xb = x2.astype(BF16)
    q, k3, _, v3, _, zs, ab, gates = _shared_projections(xb, wts)
    (xc,) = _project(xb, wts["gd"], [F32])
    qh = q.reshape(B, T, DA_KV_HEADS, DA_REP, 2, DA_HEAD_DIM).transpose(0, 2, 4, 3, 1, 5)
    qh = qh.reshape(B, 2 * DA_KV_HEADS, DA_REP * T, DA_HEAD_DIM)
    qh = jnp.pad(qh, ((0, 0), (0, 0), (0, Q_ROWS - DA_REP * T), (0, 0)))
    as_page = lambda s: jnp.pad(s.reshape(B, T * PAGE_SLABS, DA_HEAD_DIM),
                                ((0, 0), (0, slabs - T * PAGE_SLABS), (0, 0)))
    k_new, v_new = as_page(k3), as_page(v3)
    oa = _attn_sample(qh, cache_k, cache_v, page_table, k_new, v_new, wts["subln_w"], wts["lams"],
                      lam_init, T)
    oa = oa[:, :, :DA_REP * T].reshape(B, DA_KV_HEADS, DA_REP, T, 2 * DA_HEAD_DIM)
    oa = oa.transpose(0, 3, 1, 2, 4).reshape(M, DA_HEADS * 2 * DA_HEAD_DIM).astype(BF16)
    cdim = xc.shape[1]
    xc3 = xc.reshape(B, T, cdim)
    grp = jnp.concatenate([jnp.zeros((B, HALO - (CONV_W - 1) - T, cdim), F32), state_conv, xc3], axis=1)
    grp = grp.reshape(B * HALO, cdim)
    take = lambda y: y.reshape(B, HALO, -1)[:, HALO - T:]
    qn = take(_conv_silu(grp, wts["conv_w"], 0, nk, normalize=True, scale=GD_DIM ** -0.5))
    kn = take(_conv_silu(grp, wts["conv_w"], nk, nk, normalize=True))
    vv = take(_conv_silu(grp, wts["conv_w"], 2 * nk, H * GD_DIM, normalize=False))
    to_cols = lambda y: y.reshape(B, T, GD_K_HEADS, GD_DIM).transpose(0, 2, 3, 1)
    to_heads = lambda y: y.reshape(B, T, H, GD_DIM).transpose(0, 2, 1, 3)
    lanes = lambda y: jnp.broadcast_to(y.reshape(B, T, H).transpose(0, 2, 1)[..., None], (B, H, T, GD_DIM))
    hpar = lambda p: jnp.broadcast_to(p.reshape(H, 1, 1), (H, 1, GD_DIM))
    ob, state = _gdn_sample(to_cols(qn), to_cols(kn), to_heads(vv), to_heads(zs.astype(F32)),
                            lanes(ab[:, :H]), lanes(ab[:, H:]), hpar(wts["a_log"]),
                            hpar(wts["dt_bias"]), wts["gnorm_w"], state_ssm)
    ob = ob.transpose(0, 2, 1, 3).reshape(M, H * GD_DIM).astype(BF16)
    y = _tail(x2, oa, ob, gates, wts, alpha)
    per_seq = lambda s: s.reshape((B, T) + s.shape[1:])
    return y.reshape(B, T, D), per_seq(k3), per_seq(v3), state, xc3[:, T - (CONV_W - 1):]


def _layer_weights(l, w_in, lambda_q1, lambda_k1, lambda_q2, lambda_k2, subln_w, conv_w, a_log,
                   dt_bias, gnorm_w, w_proj_a, w_proj_b, w_out, ln1_g, ln1_b, w_up, w_down,
                   ln2_g, ln2_b):
    nq = DA_HEADS * 2 * DA_HEAD_DIM
    nkv = DA_KV_HEADS * 2 * DA_HEAD_DIM
    nk = GD_K_HEADS * GD_DIM
    nv = GD_V_HEADS * GD_DIM
    b0 = nq
    b1 = b0 + 2 * nkv
    b2 = b1 + 2 * nk + nv
    b3 = b2 + nv
    b4 = b3 + 2 * GD_V_HEADS
    row = lambda p: p[l].reshape(1, -1)
    cols = lambda c0, n: _Weights(w_in, n, l, c0)
    tail = lambda c0, c1: _Weights(w_in[l][:, c0:c1].astype(BF16), c1 - c0)
    return {
        "q": cols(0, nq), "k": cols(b0, nkv), "v": cols(b0 + nkv, nkv),
        "gq": cols(b1, nk), "gk": cols(b1 + nk, nk), "gv": cols(b1 + 2 * nk, nv),
        "gd": cols(b1, 2 * nk + nv), "z": cols(b2, nv),
        "ab": tail(b3, b4), "gates": tail(b4, w_in.shape[2]),
        "lams": (row(lambda_q1), row(lambda_k1), row(lambda_q2), row(lambda_k2)),
        "subln_w": row(subln_w), "conv_w": conv_w[l], "a_log": a_log[l], "dt_bias": dt_bias[l],
        "gnorm_w": row(gnorm_w),
        "w_proj_a": w_proj_a[l].astype(BF16), "w_proj_b": w_proj_b[l].astype(BF16),
        "w_out": w_out[l].astype(BF16), "ln1_g": row(ln1_g), "ln1_b": row(ln1_b),
        "w_up": w_up[l].astype(BF16), "w_down": w_down[l].astype(BF16),
        "ln2_g": row(ln2_g), "ln2_b": row(ln2_b),
    }


def kernel(x_prompt, x_sample, cache_k, cache_v, page_table, state_ssm, state_conv, w_in, lambda_q1, lambda_k1, lambda_q2, lambda_k2, subln_w, conv_w, a_log, dt_bias, gnorm_w, w_proj_a, w_proj_b, w_out, ln1_g, ln1_b, w_up, w_down, ln2_g, ln2_b):
    depth = w_in.shape[0]
    alpha = (2.0 * depth) ** 0.25
    bp, seq, d = x_prompt.shape
    n_pool, page = cache_k.shape[1], cache_k.shape[2]
    slab_view = (depth * n_pool, page * PAGE_SLABS, DA_HEAD_DIM)
    ck = cache_k.reshape(slab_view)
    cv = cache_v.reshape(depth, n_pool, page, DA_KV_HEADS, 2, DA_HEAD_DIM)
    cv = cv.transpose(0, 1, 2, 4, 3, 5).reshape(slab_view)
    yp, ys = x_prompt, x_sample
    outs = [[] for _ in range(8)]
    for l in range(depth):
        wts = _layer_weights(l, w_in, lambda_q1, lambda_k1, lambda_q2, lambda_k2, subln_w, conv_w,
                             a_log, dt_bias, gnorm_w, w_proj_a, w_proj_b, w_out, ln1_g, ln1_b,
                             w_up, w_down, ln2_g, ln2_b)
        lam_init = _lambda_init(l)
        per_seq = [_layer_prompt(yp[i], wts, lam_init, alpha) for i in range(bp)]
        yp, kp, vp, sp, cp = (_stack(t) for t in zip(*per_seq))
        ys, k_s, v_s, s_s, c_s = _layer_sample(
            ys, ck, cv, page_table + l * n_pool, state_ssm[l], state_conv[l], wts, lam_init, alpha)
        k_out = lambda s: s.reshape(s.shape[:2] + (DA_KV_HEADS, 2, DA_HEAD_DIM))
        v_out = lambda s: (s.reshape(s.shape[:2] + (2, DA_KV_HEADS, DA_HEAD_DIM))
                           .transpose(0, 1, 3, 2, 4).reshape(s.shape[:2] + (DA_KV_HEADS, -1)))
        for lst, val in zip(outs, (k_out(kp), v_out(vp), sp, cp, k_out(k_s), v_out(v_s), s_s, c_s)):
            lst.append(val)
    return (yp, ys) + tuple(_stack(o) for o in outs)
```

```python
import functools
import math

import jax
import jax.numpy as jnp
from jax import lax
from jax.experimental import pallas as pl
from jax.experimental.pallas import tpu as pltpu

F32 = jnp.float32
BF16 = jnp.bfloat16

DA_HEADS = 8
DA_KV_HEADS = 4
DA_HEAD_DIM = 128
DA_REP = DA_HEADS // DA_KV_HEADS
GD_K_HEADS = 16
GD_V_HEADS = 32
GD_DIM = 128
GD_REP = GD_V_HEADS // GD_K_HEADS
CONV_W = 4
CHUNK = 64
NORM_EPS = 1e-5
L2_EPS = 1e-6
NEG_INF = -1e30

V7X_VMEM_BYTES = 64 * 1024 * 1024
VMEM_LIMIT = V7X_VMEM_BYTES - 12 * 1024 * 1024

_NT = (((1,), (1,)), ((), ()))
_NN = (((1,), (0,)), ((), ()))


def _params(*sem):
    return pltpu.CompilerParams(dimension_semantics=sem, vmem_limit_bytes=VMEM_LIMIT)


def _dot(a, b, dn=_NN):
    return lax.dot_general(a, b, dn, preferred_element_type=F32)


def _split(a):
    hi = a.astype(BF16)
    lo = (a - hi.astype(F32)).astype(BF16)
    return hi, lo


def _dot3(a, b, dn=_NN):
    ah, al = _split(a)
    bh, bl = _split(b)
    return _dot(ah, bh, dn) + (_dot(al, bh, dn) + _dot(ah, bl, dn))


def _softplus(x):
    return jnp.maximum(x, 0.0) + jnp.log(1.0 + jnp.exp(-jnp.abs(x)))


def _sigmoid(x):
    return 1.0 / (1.0 + jnp.exp(-x))


def _lambda(lq1, lk1, lq2, lk2, lam_init):
    a = jnp.sum(lq1[...] * lk1[...], axis=-1, keepdims=True)
    b = jnp.sum(lq2[...] * lk2[...], axis=-1, keepdims=True)
    return jnp.exp(a) - jnp.exp(b) + lam_init


def _layer_norm(x, g, b):
    mu = jnp.mean(x, axis=-1, keepdims=True)
    xc = x - mu
    var = jnp.mean(xc * xc, axis=-1, keepdims=True)
    return xc * lax.rsqrt(var + NORM_EPS) * g + b


HALO = 8


class _Weights:
    def __init__(self, wt, ncols, layer=None, col0=0):
        self.w, self.ncols, self.layer, self.col0 = wt, ncols, layer, col0
        self.cast = wt.dtype != BF16

    def spec(self, K, tn):
        if self.w.ndim == 2:
            assert self.col0 == 0
            return pl.BlockSpec((tn, K), lambda j, i: (j, 0))
        off, layer = self.col0 // tn, self.layer
        assert self.col0 % tn == 0
        return pl.BlockSpec((None, tn, K), lambda j, i: (layer, j + off, 0),
                            pipeline_mode=pl.Buffered(1))

    def scratch(self, K, tn):
        return [pltpu.VMEM((tn, K), BF16)] if self.cast else []


def _project_tile(x_ref, w_ref, scr, cast):
    if not cast:
        return _dot(x_ref[...], w_ref[...], _NT)
    wb_sc = scr[0]

    @pl.when(pl.program_id(1) == 0)
    def _():
        wb_sc[...] = w_ref[...].astype(BF16)

    return _dot(x_ref[...], wb_sc[...], _NT)


def _proj_kernel(x_ref, w_ref, *rest, n_out, cast, scale, act):
    acc = _project_tile(x_ref, w_ref, rest[n_out:], cast)
    if scale != 1.0:
        acc = acc * scale
    if act == "silu":
        acc = acc * _sigmoid(acc)
    elif act == "sigmoid":
        acc = _sigmoid(acc)
    for o in rest[:n_out]:
        o[...] = acc.astype(o.dtype)


def _project(x, wt, out_dtypes, *, scale=1.0, act=None, tm=1024, tn=1024):
    M, K = x.shape
    N = wt.ncols
    tm = min(tm, M)
    tn = min(tn, N)
    return pl.pallas_call(
        functools.partial(_proj_kernel, n_out=len(out_dtypes), cast=wt.cast, scale=scale, act=act),
        grid=(N // tn, M // tm),
        in_specs=[pl.BlockSpec((tm, K), lambda j, i: (i, 0)), wt.spec(K, tn)],
        out_specs=[pl.BlockSpec((tm, tn), lambda j, i: (i, j)) for _ in out_dtypes],
        out_shape=[jax.ShapeDtypeStruct((M, N), d) for d in out_dtypes],
        scratch_shapes=wt.scratch(K, tn),
        compiler_params=_params("parallel", "arbitrary"),
        name="in_proj",
    )(x, wt.w)


def _proj_slab_kernel(x_ref, w_ref, o3_ref, ob_ref, *scr, cast, order):
    acc = _project_tile(x_ref, w_ref, scr, cast)
    ob_ref[...] = acc.astype(ob_ref.dtype)
    for slab, cb in enumerate(order):
        o3_ref[:, slab, :] = acc[:, cb * DA_HEAD_DIM:(cb + 1) * DA_HEAD_DIM]


def _project_slabs(x, wt, order, *, tm=1024):
    M, K = x.shape
    N = wt.ncols
    tm = min(tm, M)
    return pl.pallas_call(
        functools.partial(_proj_slab_kernel, cast=wt.cast, order=order),
        grid=(1, M // tm),
        in_specs=[pl.BlockSpec((tm, K), lambda j, i: (i, 0)), wt.spec(K, N)],
        out_specs=[pl.BlockSpec((tm, len(order), DA_HEAD_DIM), lambda j, i: (i, 0, 0)),
                   pl.BlockSpec((tm, N), lambda j, i: (i, 0))],
        out_shape=[jax.ShapeDtypeStruct((M, len(order), DA_HEAD_DIM), F32),
                   jax.ShapeDtypeStruct((M, N), BF16)],
        scratch_shapes=wt.scratch(K, N),
        compiler_params=_params("parallel", "arbitrary"),
        name="kv_proj",
    )(x, wt.w)


def _conv_silu_rows(xs_sc, cw_ref, o_ref, *, tm, tc, normalize, scale):
    y = cw_ref[CONV_W - 1:CONV_W, :] * xs_sc[HALO:HALO + tm, :]
    for back in range(1, CONV_W):
        y = y + cw_ref[CONV_W - 1 - back:CONV_W - back, :] * xs_sc[HALO - back:HALO - back + tm, :]
    y = y * _sigmoid(y)
    if normalize:
        for h in range(tc // GD_DIM):
            ys = y[:, h * GD_DIM:(h + 1) * GD_DIM]
            ss = jnp.sum(ys * ys, axis=-1, keepdims=True)
            o_ref[:, h * GD_DIM:(h + 1) * GD_DIM] = ys * (lax.rsqrt(ss + L2_EPS) * scale)
    else:
        o_ref[...] = y
    xs_sc[0:HALO, :] = xs_sc[tm:tm + HALO, :]


def _proj_conv_kernel(x_ref, w_ref, cw_ref, o_ref, tail_ref, *scr, tm, tn, cast, normalize, scale):
    xs_sc = scr[-1]

    @pl.when(pl.program_id(1) == 0)
    def _():
        xs_sc[0:HALO, :] = jnp.zeros((HALO, tn), F32)

    xs_sc[HALO:HALO + tm, :] = _project_tile(x_ref, w_ref, scr, cast)
    tail_ref[...] = xs_sc[tm:tm + HALO, :]
    _conv_silu_rows(xs_sc, cw_ref, o_ref, tm=tm, tc=tn, normalize=normalize, scale=scale)


def _project_conv(x, wt, conv_w, conv_col0, *, normalize, scale=1.0, tm=1024, tn=1024):
    M, K = x.shape
    N = wt.ncols
    tm = min(tm, M)
    tn = min(tn, N)
    coff = conv_col0 // tn
    assert conv_col0 % tn == 0
    return pl.pallas_call(
        functools.partial(_proj_conv_kernel, tm=tm, tn=tn, cast=wt.cast, normalize=normalize,
                          scale=scale),
        grid=(N // tn, M // tm),
        in_specs=[pl.BlockSpec((tm, K), lambda j, i: (i, 0)), wt.spec(K, tn),
                  pl.BlockSpec((CONV_W, tn), lambda j, i: (0, j + coff))],
        out_specs=[pl.BlockSpec((tm, tn), lambda j, i: (i, j)),
                   pl.BlockSpec((HALO, tn), lambda j, i: (0, j))],
        out_shape=[jax.ShapeDtypeStruct((M, N), F32), jax.ShapeDtypeStruct((HALO, N), F32)],
        scratch_shapes=wt.scratch(K, tn) + [pltpu.VMEM((tm + HALO, tn), F32)],
        compiler_params=_params("parallel", "arbitrary"),
        name="gd_proj_conv",
    )(x, wt.w, conv_w)


def _subln_out(acc0, l0, acc1, l1, lam, subw, lam_init):
    o = acc0 / l0 - lam * (acc1 / l1)
    o = o * lax.rsqrt(jnp.mean(o * o, axis=-1, keepdims=True) + NORM_EPS)
    return o * subw * (1.0 - lam_init)


def _attn_prompt_kernel(q_ref, k_ref, v_ref, subw_ref, lq1, lk1, lq2, lk2, o_ref,
                        m_sc, l_sc, acc_sc, *, tq, tk, lam_init):
    qi = pl.program_id(1)
    ki = pl.program_id(2)
    last = (qi * tq + tq - 1) // tk

    @pl.when(ki == 0)
    def _():
        m_sc[...] = jnp.full(m_sc.shape, -jnp.inf, F32)
        l_sc[...] = jnp.zeros(l_sc.shape, F32)
        acc_sc[...] = jnp.zeros(acc_sc.shape, F32)

    def step(masked):
        v = v_ref[...]
        if masked:
            ahead = (lax.broadcasted_iota(jnp.int32, (tq, tk), 1)
                     - lax.broadcasted_iota(jnp.int32, (tq, tk), 0))
            keep = ahead <= qi * tq - ki * tk
        for r in range(DA_REP):
            for c in range(2):
                idx = r * 2 + c
                q = q_ref[:, idx * DA_HEAD_DIM:(idx + 1) * DA_HEAD_DIM]
                k = k_ref[:, c * DA_HEAD_DIM:(c + 1) * DA_HEAD_DIM]
                s = _dot(q, k, _NT)
                if masked:
                    s = jnp.where(keep, s, NEG_INF)
                m_prev = m_sc[idx]
                m_new = jnp.maximum(m_prev, jnp.max(s, axis=-1, keepdims=True))
                alpha = jnp.exp2(m_prev - m_new)
                p = jnp.exp2(s - m_new)
                l_sc[idx] = alpha * l_sc[idx] + jnp.sum(p, axis=-1, keepdims=True)
                acc_sc[idx] = alpha * acc_sc[idx] + _dot(p.astype(BF16), v)
                m_sc[idx] = m_new

    @pl.when(ki < last)
    def _():
        step(False)

    @pl.when(ki == last)
    def _():
        step(True)
        lam = _lambda(lq1, lk1, lq2, lk2, lam_init)
        e = 2 * DA_HEAD_DIM
        for r in range(DA_REP):
            o = _subln_out(acc_sc[r * 2], l_sc[r * 2], acc_sc[r * 2 + 1], l_sc[r * 2 + 1],
                           lam, subw_ref[...], lam_init)
            o_ref[:, r * e:(r + 1) * e] = o.astype(o_ref.dtype)


def _attn_prompt(q, k, v, subw, lams, lam_init, *, tq=512, tk=2048):
    L = q.shape[0]
    tq = min(tq, L)
    tk = min(tk, L)
    assert tk % tq == 0
    e = 2 * DA_HEAD_DIM
    vec = pl.BlockSpec((1, DA_HEAD_DIM), lambda g, i, j: (0, 0))
    kblk = lambda i, j: jnp.minimum(j, (i * tq + tq - 1) // tk)
    return pl.pallas_call(
        functools.partial(_attn_prompt_kernel, tq=tq, tk=tk, lam_init=lam_init),
        grid=(DA_KV_HEADS, L // tq, L // tk),
        in_specs=[pl.BlockSpec((tq, DA_REP * e), lambda g, i, j: (i, g)),
                  pl.BlockSpec((tk, e), lambda g, i, j: (kblk(i, j), g)),
                  pl.BlockSpec((tk, e), lambda g, i, j: (kblk(i, j), g)),
                  pl.BlockSpec((1, e), lambda g, i, j: (0, 0)),
                  vec, vec, vec, vec],
        out_specs=pl.BlockSpec((tq, DA_REP * e), lambda g, i, j: (i, g)),
        out_shape=jax.ShapeDtypeStruct((L, DA_HEADS * e), BF16),
        scratch_shapes=[pltpu.VMEM((2 * DA_REP, tq, 1), F32),
                        pltpu.VMEM((2 * DA_REP, tq, 1), F32),
                        pltpu.VMEM((2 * DA_REP, tq, e), F32)],
        compiler_params=_params("parallel", "parallel", "arbitrary"),
        name="attn_prompt",
    )(q, k, v, subw, *lams)


Q_ROWS = 16


PAGE_SLABS = 2 * DA_KV_HEADS
PAGES_PER_STEP = 16


def _attn_sample_kernel(pt_ref, q_ref, *rest, pages, page, dec_seq, lam_init):
    k_refs = rest[:pages]
    v_refs = rest[pages:2 * pages]
    kn_ref, vn_ref, subw_ref, lq1, lk1, lq2, lk2, o_ref, m_sc, l_sc, acc_sc = rest[2 * pages:]
    j = pl.program_id(1)
    rows = 2 * DA_KV_HEADS * Q_ROWS

    @pl.when(j == 0)
    def _():
        m_sc[...] = jnp.full(m_sc.shape, -jnp.inf, F32)
        l_sc[...] = jnp.zeros(l_sc.shape, F32)
        acc_sc[...] = jnp.zeros(acc_sc.shape, F32)

    def slab(ref, i):
        return ref[0, pl.ds(i, page, stride=PAGE_SLABS), :]

    def update(k_list, v_list, keep):
        s = []
        for idx in range(2 * DA_KV_HEADS):
            k = jnp.concatenate([slab(kr, idx) for kr in k_list], axis=0).astype(BF16)
            s.append(_dot(q_ref[0, idx], k, _NT))
        s = jnp.concatenate(s, axis=0)
        if keep is not None:
            s = jnp.where(keep, s, NEG_INF)
        m_prev = m_sc[...]
        m_new = jnp.maximum(m_prev, jnp.max(s, axis=-1, keepdims=True))
        alpha = jnp.exp2(m_prev - m_new)
        p = jnp.exp2(s - m_new)
        l_sc[...] = alpha * l_sc[...] + jnp.sum(p, axis=-1, keepdims=True)
        m_sc[...] = m_new
        p = p.astype(BF16)
        pv = []
        for g in range(DA_KV_HEADS):
            v = jnp.concatenate(
                [jnp.concatenate([slab(vr, g), slab(vr, DA_KV_HEADS + g)], axis=1) for vr in v_list],
                axis=0).astype(BF16)
            pv.append(_dot(p[g * 2 * Q_ROWS:(g + 1) * 2 * Q_ROWS], v))
        acc_sc[...] = alpha * acc_sc[...] + jnp.concatenate(pv, axis=0)

    update(k_refs, v_refs, None)

    @pl.when(j == pl.num_programs(1) - 1)
    def _():
        row = lax.broadcasted_iota(jnp.int32, (rows, page), 0)
        col = lax.broadcasted_iota(jnp.int32, (rows, page), 1)
        update([kn_ref], [vn_ref], col <= (row % Q_ROWS) % dec_seq)
        lam = _lambda(lq1, lk1, lq2, lk2, lam_init)
        for g in range(DA_KV_HEADS):
            r0 = slice(2 * g * Q_ROWS, (2 * g + 1) * Q_ROWS)
            r1 = slice((2 * g + 1) * Q_ROWS, (2 * g + 2) * Q_ROWS)
            o = _subln_out(acc_sc[r0, :], l_sc[r0, :], acc_sc[r1, :], l_sc[r1, :],
                           lam, subw_ref[...], lam_init)
            o_ref[0, g] = o.astype(o_ref.dtype)


def _attn_sample(q, cache_k, cache_v, page_table, k_new, v_new, subw, lams, lam_init, dec_seq):
    B = q.shape[0]
    n_pages = page_table.shape[1]
    page = cache_k.shape[1] // PAGE_SLABS
    pages = math.gcd(n_pages, PAGES_PER_STEP)
    e = 2 * DA_HEAD_DIM
    rows = 2 * DA_KV_HEADS * Q_ROWS

    def page_spec(i):
        return pl.BlockSpec((1, page * PAGE_SLABS, DA_HEAD_DIM),
                            lambda b, j, pt: (pt[b, j * pages + i], 0, 0))

    vec = pl.BlockSpec((1, DA_HEAD_DIM), lambda b, j, pt: (0, 0))
    new_spec = pl.BlockSpec((1, page * PAGE_SLABS, DA_HEAD_DIM), lambda b, j, pt: (b, 0, 0))
    grid_spec = pltpu.PrefetchScalarGridSpec(
        num_scalar_prefetch=1,
        grid=(B, n_pages // pages),
        in_specs=([pl.BlockSpec((1, 2 * DA_KV_HEADS, Q_ROWS, DA_HEAD_DIM), lambda b, j, pt: (b, 0, 0, 0))]
                  + [page_spec(i) for i in range(pages)] + [page_spec(i) for i in range(pages)]
                  + [new_spec, new_spec, pl.BlockSpec((1, e), lambda b, j, pt: (0, 0)),
                     vec, vec, vec, vec]),
        out_specs=pl.BlockSpec((1, DA_KV_HEADS, Q_ROWS, e), lambda b, j, pt: (b, 0, 0, 0)),
        scratch_shapes=[pltpu.VMEM((rows, 1), F32),
                        pltpu.VMEM((rows, 1), F32),
                        pltpu.VMEM((rows, e), F32)])
    return pl.pallas_call(
        functools.partial(_attn_sample_kernel, pages=pages, page=page, dec_seq=dec_seq,
                          lam_init=lam_init),
        grid_spec=grid_spec,
        out_shape=jax.ShapeDtypeStruct((B, DA_KV_HEADS, Q_ROWS, e), F32),
        compiler_params=_params("parallel", "arbitrary"),
        name="attn_sample",
    )(page_table, q, *([cache_k] * pages), *([cache_v] * pages), k_new, v_new, subw, *lams)


def _conv_kernel(x_ref, w_ref, o_ref, xs_sc, *, tm, tc, normalize, scale):
    @pl.when(pl.program_id(1) == 0)
    def _():
        xs_sc[0:HALO, :] = jnp.zeros((HALO, tc), F32)

    xs_sc[HALO:HALO + tm, :] = x_ref[...]
    _conv_silu_rows(xs_sc, w_ref, o_ref, tm=tm, tc=tc, normalize=normalize, scale=scale)


def _conv_silu(x, w, col0, ncols, *, normalize, scale=1.0, tm=512, tc=512):
    L = x.shape[0]
    tm = min(tm, L)
    off = col0 // tc
    return pl.pallas_call(
        functools.partial(_conv_kernel, tm=tm, tc=tc, normalize=normalize, scale=scale),
        grid=(ncols // tc, L // tm),
        in_specs=[pl.BlockSpec((tm, tc), lambda c, i: (i, c + off)),
                  pl.BlockSpec((CONV_W, tc), lambda c, i: (0, c + off))],
        out_specs=pl.BlockSpec((tm, tc), lambda c, i: (i, c)),
        out_shape=jax.ShapeDtypeStruct((L, ncols), F32),
        scratch_shapes=[pltpu.VMEM((tm + HALO, tc), F32)],
        compiler_params=_params("parallel", "arbitrary"),
        name="conv_silu",
    )(x, w)


GROUP_CHUNKS = 4
SCAN_HEADS = 4


def _gdn_intra_kernel(q_ref, k_ref, v_ref, arow_ref, brow_ref, alog_ref, dtb_ref,
                      mq_ref, bo_ref, egl_ref, *, groups, C, cpg):
    gt = cpg * C
    shift = C.bit_length() - 1
    row = lax.broadcasted_iota(jnp.int32, (gt, gt), 0)
    col = lax.broadcasted_iota(jnp.int32, (gt, gt), 1)
    same = (row >> shift) == (col >> shift)
    tril = same & (col <= row)
    strict = same & (col < row)
    diag = row == col
    eye = jnp.where(diag, 1.0, 0.0)
    r2 = lax.broadcasted_iota(jnp.int32, (cpg * GD_DIM, gt), 0)
    c2 = lax.broadcasted_iota(jnp.int32, (cpg * GD_DIM, gt), 1)
    ke_mask = (r2 >> (GD_DIM.bit_length() - 1)) == (c2 >> shift)
    units = []
    for gi in range(groups):
        rows = slice(gi * gt, (gi + 1) * gt)
        q = q_ref[rows, :]
        k = k_ref[rows, :]
        k16 = k.astype(BF16)
        raw = _dot(jnp.concatenate([q.astype(BF16), k16], axis=0), k16, _NT)
        for j in range(GD_REP):
            neg_a = -jnp.exp(alog_ref[j])
            dtb = dtb_ref[j]
            g_row = neg_a * _softplus(arow_ref[j, gi] + dtb)
            beta = jnp.sum(jnp.where(diag, _sigmoid(brow_ref[j, gi]), 0.0), axis=1, keepdims=True)
            gc_col = jnp.sum(jnp.where(tril, g_row, 0.0), axis=1, keepdims=True)
            gc_row = jnp.sum(jnp.where(diag, gc_col, 0.0), axis=0, keepdims=True)
            gl_col = jnp.sum(jnp.where(same, g_row, 0.0), axis=1, keepdims=True)
            decay = jnp.where(tril, jnp.exp(jnp.where(tril, gc_col - gc_row, 0.0)), 0.0)
            lm = jnp.where(strict, beta * raw[gt:] * decay, 0.0)
            units.append(dict(gi=gi, j=j, rows=rows, beta=beta, gc_col=gc_col, gl_col=gl_col,
                              qk=raw[:gt] * decay, x=eye - lm, p16=lm.astype(BF16)))
    n = 1
    while 2 * n < C:
        for u in units:
            u["p16"] = _dot(u["p16"], u["p16"]).astype(BF16)
        for u in units:
            u["x"] = u["x"] + _dot(u["x"].astype(BF16), u["p16"])
        n *= 2
    for u in units:
        k = k_ref[u["rows"], :]
        v = v_ref[u["rows"], u["j"] * GD_DIM:(u["j"] + 1) * GD_DIM]
        u["eg"] = jnp.exp(u["gc_col"])
        u["wu"] = _dot(u["x"].astype(BF16),
                       jnp.concatenate([k * (u["beta"] * u["eg"]), v * u["beta"]], axis=1).astype(BF16))
    for u in units:
        gi, j = u["gi"], u["j"]
        q = q_ref[u["rows"], :]
        k = k_ref[u["rows"], :]
        ket = (k * jnp.exp(u["gl_col"] - u["gc_col"])).T
        ke_bd = jnp.where(ke_mask, jnp.concatenate([ket] * cpg, axis=0), 0.0)
        res = _dot(jnp.concatenate([ke_bd, u["qk"]], axis=0).astype(BF16), u["wu"].astype(BF16))
        qe = q * u["eg"]
        for c in range(cpg):
            n_idx = gi * cpg + c
            top = slice(c * GD_DIM, (c + 1) * GD_DIM)
            bot = slice(cpg * GD_DIM + c * C, cpg * GD_DIM + (c + 1) * C)
            mq_ref[j, n_idx, 0:GD_DIM, :] = res[top, :GD_DIM].astype(BF16)
            mq_ref[j, n_idx, GD_DIM:GD_DIM + C, :] = (
                qe[c * C:(c + 1) * C] - res[bot, :GD_DIM]).astype(BF16)
            bo_ref[j, n_idx, 0:GD_DIM, :] = res[top, GD_DIM:]
            bo_ref[j, n_idx, GD_DIM:GD_DIM + C, :] = res[bot, GD_DIM:]
            egl_ref[j, n_idx] = jnp.broadcast_to(
                jnp.exp(u["gl_col"][c * C:c * C + 1, :]), (1, GD_DIM))


def _gdn_scan_kernel(mq_ref, bo_ref, egl_ref, zs_ref, gw_ref, o_ref, s_ref, s_sc, *, nb, C, hb):
    n = pl.program_id(1)

    @pl.when(n == 0)
    def _():
        s_sc[...] = jnp.zeros(s_sc.shape, F32)

    s = [s_sc[h] for h in range(hb)]
    for c in range(nb):
        rows = slice(c * C, (c + 1) * C)
        for h in range(hb):
            cols = slice(h * GD_DIM, (h + 1) * GD_DIM)
            r = _dot(mq_ref[h, c], s[h].astype(BF16))
            s[h] = egl_ref[h, c] * s[h] - r[:GD_DIM] + bo_ref[h, c, 0:GD_DIM, :]
            o = r[GD_DIM:] + bo_ref[h, c, GD_DIM:GD_DIM + C, :]
            o = o * lax.rsqrt(jnp.mean(o * o, axis=-1, keepdims=True) + NORM_EPS) * gw_ref[...]
            o_ref[rows, cols] = (o * zs_ref[rows, cols].astype(F32)).astype(o_ref.dtype)
    for h in range(hb):
        s_sc[h] = s[h]

    @pl.when(n == pl.num_programs(1) - 1)
    def _():
        s_ref[...] = s_sc[...]


def _gdn_prompt(qn, kn, vv, zs, a, b, a_log, dt_bias, gnorm_w, *, groups=4):
    L = qn.shape[0]
    H = GD_V_HEADS
    C = min(CHUNK, L)
    assert C & (C - 1) == 0 and L % C == 0
    N = L // C
    cpg = math.gcd(GROUP_CHUNKS, N)
    gt = cpg * C
    ng = N // cpg
    groups = math.gcd(groups, ng)
    nb = groups * cpg
    hb = SCAN_HEADS
    a_t = a.T.reshape(H, ng, gt)
    b_t = b.T.reshape(H, ng, gt)
    hvec = lambda x: x.reshape(H, 1, 1)
    grp = lambda *shape: pl.BlockSpec((GD_REP, groups) + shape, lambda hk, n: (hk, n, 0, 0))
    per_chunk = lambda heads, *shape: pl.BlockSpec((heads, nb) + shape, lambda h, n: (h, n, 0, 0))
    mq, bo, egl = pl.pallas_call(
        functools.partial(_gdn_intra_kernel, groups=groups, C=C, cpg=cpg),
        grid=(GD_K_HEADS, ng // groups),
        in_specs=[pl.BlockSpec((groups * gt, GD_DIM), lambda hk, n: (n, hk)),
                  pl.BlockSpec((groups * gt, GD_DIM), lambda hk, n: (n, hk)),
                  pl.BlockSpec((groups * gt, GD_REP * GD_DIM), lambda hk, n: (n, hk)),
                  grp(1, gt), grp(1, gt),
                  pl.BlockSpec((GD_REP, 1, 1), lambda hk, n: (hk, 0, 0)),
                  pl.BlockSpec((GD_REP, 1, 1), lambda hk, n: (hk, 0, 0))],
        out_specs=[per_chunk(GD_REP, GD_DIM + C, GD_DIM), per_chunk(GD_REP, GD_DIM + C, GD_DIM),
                   per_chunk(GD_REP, 1, GD_DIM)],
        out_shape=[jax.ShapeDtypeStruct((H, N, GD_DIM + C, GD_DIM), BF16),
                   jax.ShapeDtypeStruct((H, N, GD_DIM + C, GD_DIM), F32),
                   jax.ShapeDtypeStruct((H, N, 1, GD_DIM), F32)],
        compiler_params=_params("parallel", "parallel"),
        name="gdn_intra",
    )(qn, kn, vv, a_t[:, :, None, :], b_t[:, :, None, :], hvec(a_log), hvec(dt_bias))
    out, state = pl.pallas_call(
        functools.partial(_gdn_scan_kernel, nb=nb, C=C, hb=hb),
        grid=(H // hb, N // nb),
        in_specs=[per_chunk(hb, GD_DIM + C, GD_DIM), per_chunk(hb, GD_DIM + C, GD_DIM),
                  per_chunk(hb, 1, GD_DIM),
                  pl.BlockSpec((nb * C, hb * GD_DIM), lambda h, n: (n, h)),
                  pl.BlockSpec((1, GD_DIM), lambda h, n: (0, 0))],
        out_specs=[pl.BlockSpec((nb * C, hb * GD_DIM), lambda h, n: (n, h)),
                   pl.BlockSpec((hb, GD_DIM, GD_DIM), lambda h, n: (h, 0, 0))],
        out_shape=[jax.ShapeDtypeStruct((L, H * GD_DIM), BF16),
                   jax.ShapeDtypeStruct((H, GD_DIM, GD_DIM), F32)],
        scratch_shapes=[pltpu.VMEM((hb, GD_DIM, GD_DIM), F32)],
        compiler_params=_params("parallel", "arbitrary"),
        name="gdn_scan",
    )(mq, bo, egl, zs, gnorm_w)
    return out, state


def _gdn_sample_kernel(qt_ref, kt_ref, v_ref, zs_ref, a_ref, b_ref, alog_ref, dtb_ref, gw_ref,
                       s_ref, o_ref, so_ref, *, T):
    def key_head(hk, carry):
        qt = qt_ref[0, hk]
        kt = kt_ref[0, hk]
        heads = [hk * GD_REP + j for j in range(GD_REP)]
        s = [s_ref[0, h] for h in heads]
        g = [-jnp.exp(alog_ref[h]) * _softplus(a_ref[0, h] + dtb_ref[h]) for h in heads]
        beta = [_sigmoid(b_ref[0, h]) for h in heads]
        v = [v_ref[0, h] for h in heads]
        outs = [[] for _ in heads]
        for t in range(T):
            kc = kt[:, t:t + 1]
            qc = qt[:, t:t + 1]
            for j in range(GD_REP):
                sj = s[j] * jnp.exp(g[j][t:t + 1, :])
                v_new = beta[j][t:t + 1, :] * (v[j][t:t + 1, :] - jnp.sum(kc * sj, axis=0, keepdims=True))
                s[j] = sj + kc * v_new
                outs[j].append(jnp.sum(qc * s[j], axis=0, keepdims=True))
        for j, h in enumerate(heads):
            so_ref[0, h] = s[j]
            o = jnp.concatenate(outs[j], axis=0)
            o = o * lax.rsqrt(jnp.mean(o * o, axis=-1, keepdims=True) + NORM_EPS) * gw_ref[...]
            o_ref[0, h] = o * zs_ref[0, h]
        return carry

    lax.fori_loop(0, GD_K_HEADS, key_head, 0)


def _gdn_sample(qt, kt, v, z, a, b, a_log, dt_bias, gnorm_w, state):
    B, H, T, _ = v.shape
    per_b = lambda *shape: pl.BlockSpec((1,) + shape, lambda i: (i,) + (0,) * len(shape))
    hpar = pl.BlockSpec((H, 1, GD_DIM), lambda i: (0, 0, 0))
    return pl.pallas_call(
        functools.partial(_gdn_sample_kernel, T=T),
        grid=(B,),
        in_specs=[per_b(GD_K_HEADS, GD_DIM, T), per_b(GD_K_HEADS, GD_DIM, T),
                  per_b(H, T, GD_DIM), per_b(H, T, GD_DIM), per_b(H, T, GD_DIM), per_b(H, T, GD_DIM),
                  hpar, hpar, pl.BlockSpec((1, GD_DIM), lambda i: (0, 0)),
                  per_b(H, GD_DIM, GD_DIM)],
        out_specs=[per_b(H, T, GD_DIM), per_b(H, GD_DIM, GD_DIM)],
        out_shape=[jax.ShapeDtypeStruct((B, H, T, GD_DIM), F32),
                   jax.ShapeDtypeStruct((B, H, GD_DIM, GD_DIM), F32)],
        compiler_params=_params("parallel"),
        name="gdn_sample",
    )(qt, kt, v, z, a, b, a_log, dt_bias, gnorm_w, state)


def _merge_kernel(oa_ref, ob_ref, wa_ref, wb_ref, ga_ref, gb_ref, o_ref):
    ya = _dot(oa_ref[...], wa_ref[...])
    yb = _dot(ob_ref[...], wb_ref[...])
    o_ref[...] = (ga_ref[...].astype(F32) * ya + gb_ref[...].astype(F32) * yb).astype(o_ref.dtype)


def _merge(oa, ob, wa, wb, gates, *, tm=512, tn=512):
    M = oa.shape[0]
    D = wa.shape[1]
    tm = min(tm, M)
    tn = min(tn, D)
    nj = D // tn
    return pl.pallas_call(
        _merge_kernel,
        grid=(nj, M // tm),
        in_specs=[pl.BlockSpec((tm, oa.shape[1]), lambda j, i: (i, 0)),
                  pl.BlockSpec((tm, ob.shape[1]), lambda j, i: (i, 0)),
                  pl.BlockSpec((wa.shape[0], tn), lambda j, i: (0, j)),
                  pl.BlockSpec((wb.shape[0], tn), lambda j, i: (0, j)),
                  pl.BlockSpec((tm, tn), lambda j, i: (i, j)),
                  pl.BlockSpec((tm, tn), lambda j, i: (i, j + nj))],
        out_specs=pl.BlockSpec((tm, tn), lambda j, i: (i, j)),
        out_shape=jax.ShapeDtypeStruct((M, D), BF16),
        compiler_params=_params("parallel", "parallel"),
        name="merge",
    )(oa, ob, wa, wb, gates, gates)


def _outproj_kernel(m_ref, w_ref, x_ref, g_ref, b_ref, o_ref, *, alpha):
    mix = _dot(m_ref[...], w_ref[...])
    o_ref[...] = _layer_norm(alpha * x_ref[...] + mix, g_ref[...], b_ref[...])


def _outproj_ln(merged, w, x, g, b, alpha, *, tm=256):
    M, D = x.shape
    tm = min(tm, M)
    row = pl.BlockSpec((1, D), lambda i: (0, 0))
    return pl.pallas_call(
        functools.partial(_outproj_kernel, alpha=alpha),
        grid=(M // tm,),
        in_specs=[pl.BlockSpec((tm, D), lambda i: (i, 0)),
                  pl.BlockSpec((D, D), lambda i: (0, 0)),
                  pl.BlockSpec((tm, D), lambda i: (i, 0)), row, row],
        out_specs=pl.BlockSpec((tm, D), lambda i: (i, 0)),
        out_shape=jax.ShapeDtypeStruct((M, D), F32),
        compiler_params=_params("parallel"),
        name="outproj_ln",
    )(merged, w, x, g, b)


def _ffn_kernel(h_ref, wu_ref, wd_ref, g_ref, b_ref, o_ref, hb_sc, acc_sc, *, alpha):
    f = pl.program_id(1)

    @pl.when(f == 0)
    def _():
        hb_sc[...] = h_ref[...].astype(BF16)
        acc_sc[...] = jnp.zeros(acc_sc.shape, F32)

    u = jnp.maximum(_dot(hb_sc[...], wu_ref[...]), 0.0)
    acc_sc[...] += _dot((u * u).astype(BF16), wd_ref[...])

    @pl.when(f == pl.num_programs(1) - 1)
    def _():
        o_ref[...] = _layer_norm(alpha * h_ref[...] + acc_sc[...], g_ref[...], b_ref[...])


def _ffn_ln(h, wu, wd, g, b, alpha, *, tm=512, tf=512):
    M, D = h.shape
    F = wu.shape[1]
    tm = min(tm, M)
    tf = min(tf, F)
    row = pl.BlockSpec((1, D), lambda i, f: (0, 0))
    return pl.pallas_call(
        functools.partial(_ffn_kernel, alpha=alpha),
        grid=(M // tm, F // tf),
        in_specs=[pl.BlockSpec((tm, D), lambda i, f: (i, 0)),
                  pl.BlockSpec((D, tf), lambda i, f: (0, f)),
                  pl.BlockSpec((tf, D), lambda i, f: (f, 0)), row, row],
        out_specs=pl.BlockSpec((tm, D), lambda i, f: (i, 0)),
        out_shape=jax.ShapeDtypeStruct((M, D), F32),
        scratch_shapes=[pltpu.VMEM((tm, D), BF16), pltpu.VMEM((tm, D), F32)],
        compiler_params=_params("parallel", "arbitrary"),
        name="ffn_ln",
    )(h, wu, wd, g, b)


def _stack(parts):
    return parts[0][None] if len(parts) == 1 else jnp.stack(parts)


def _lambda_init(layer):
    return 0.8 - 0.6 * math.exp(-0.3 * layer)


K_SLAB_ORDER = tuple(range(PAGE_SLABS))
V_SLAB_ORDER = tuple(g * 2 + half for half in range(2) for g in range(DA_KV_HEADS))


def _shared_projections(xb, wts):
    (q,) = _project(xb, wts["q"], [BF16], scale=DA_HEAD_DIM ** -0.5 * math.log2(math.e))
    k3, kb = _project_slabs(xb, wts["k"], K_SLAB_ORDER)
    v3, vb = _project_slabs(xb, wts["v"], V_SLAB_ORDER)
    (zs,) = _project(xb, wts["z"], [BF16], act="silu")
    (ab,) = _project(xb, wts["ab"], [F32])
    (gates,) = _project(xb, wts["gates"], [BF16], act="sigmoid")
    return q, k3, kb, v3, vb, zs, ab, gates


def _tail(x, oa, ob, gates, wts, alpha):
    merged = _merge(oa, ob, wts["w_proj_a"], wts["w_proj_b"], gates)
    h = _outproj_ln(merged, wts["w_out"], x, wts["ln1_g"], wts["ln1_b"], alpha)
    return _ffn_ln(h, wts["w_up"], wts["w_down"], wts["ln2_g"], wts["ln2_b"], alpha)


def _layer_prompt(x, wts, lam_init, alpha):
    nk = GD_K_HEADS * GD_DIM
    xb = x.astype(BF16)
    q, k3, kb, v3, vb, zs, ab, gates = _shared_projections(xb, wts)
    oa = _attn_prompt(q, kb, vb, wts["subln_w"], wts["lams"], lam_init)
    qn, tq = _project_conv(xb, wts["gq"], wts["conv_w"], 0, normalize=True, scale=GD_DIM ** -0.5)
    kn, tk = _project_conv(xb, wts["gk"], wts["conv_w"], nk, normalize=True)
    vv, tv = _project_conv(xb, wts["gv"], wts["conv_w"], 2 * nk, normalize=False)
    ob, state = _gdn_prompt(qn, kn, vv, zs, ab[:, :GD_V_HEADS], ab[:, GD_V_HEADS:],
                            wts["a_log"], wts["dt_bias"], wts["gnorm_w"])
    y = _tail(x, oa, ob, gates, wts, alpha)
    conv_rows = jnp.concatenate([tq, tk, tv], axis=1)[HALO - (CONV_W - 1):]
    return y, k3, v3, state, conv_rows


def _layer_sample(x, cache_k, cache_v, page_table, state_ssm, state_conv, wts, lam_init, alpha):
    B, T, D = x.shape
    M = B * T
    H = GD_V_HEADS
    nk = GD_K_HEADS * GD_DIM
    slabs = cache_k.shape[1]
    x2 = x.reshape(M, D)
    xb = x2.astype(BF16)
    q, k3, _, v3, _, zs, ab, gates = _shared_projections(xb, wts)
    (xc,) = _project(xb, wts["gd"], [F32])
    qh = q.reshape(B, T, DA_KV_HEADS, DA_REP, 2, DA_HEAD_DIM).transpose(0, 2, 4, 3, 1, 5)
    qh = qh.reshape(B, 2 * DA_KV_HEADS, DA_REP * T, DA_HEAD_DIM)
    qh = jnp.pad(qh, ((0, 0), (0, 0), (0, Q_ROWS - DA_REP * T), (0, 0)))
    as_page = lambda s: jnp.pad(s.reshape(B, T * PAGE_SLABS, DA_HEAD_DIM),
                                ((0, 0), (0, slabs - T * PAGE_SLABS), (0, 0)))
    k_new, v_new = as_page(k3), as_page(v3)
    oa = _attn_sample(qh, cache_k, cache_v, page_table, k_new, v_new, wts["subln_w"], wts["lams"],
                      lam_init, T)
    oa = oa[:, :, :DA_REP * T].reshape(B, DA_KV_HEADS, DA_REP, T, 2 * DA_HEAD_DIM)
    oa = oa.transpose(0, 3, 1, 2, 4).reshape(M, DA_HEADS * 2 * DA_HEAD_DIM).astype(BF16)
    cdim = xc.shape[1]
    xc3 = xc.reshape(B, T, cdim)
    grp = jnp.concatenate([jnp.zeros((B, HALO - (CONV_W - 1) - T, cdim), F32), state_conv, xc3], axis=1)
    grp = grp.reshape(B * HALO, cdim)
    take = lambda y: y.reshape(B, HALO, -1)[:, HALO - T:]
    qn = take(_conv_silu(grp, wts["conv_w"], 0, nk, normalize=True, scale=GD_DIM ** -0.5))
    kn = take(_conv_silu(grp, wts["conv_w"], nk, nk, normalize=True))
    vv = take(_conv_silu(grp, wts["conv_w"], 2 * nk, H * GD_DIM, normalize=False))
    to_cols = lambda y: y.reshape(B, T, GD_K_HEADS, GD_DIM).transpose(0, 2, 3, 1)
    to_heads = lambda y: y.reshape(B, T, H, GD_DIM).transpose(0, 2, 1, 3)
    lanes = lambda y: jnp.broadcast_to(y.reshape(B, T, H).transpose(0, 2, 1)[..., None], (B, H, T, GD_DIM))
    hpar = lambda p: jnp.broadcast_to(p.reshape(H, 1, 1), (H, 1, GD_DIM))
    ob, state = _gdn_sample(to_cols(qn), to_cols(kn), to_heads(vv), to_heads(zs.astype(F32)),
                            lanes(ab[:, :H]), lanes(ab[:, H:]), hpar(wts["a_log"]),
                            hpar(wts["dt_bias"]), wts["gnorm_w"], state_ssm)
    ob = ob.transpose(0, 2, 1, 3).reshape(M, H * GD_DIM).astype(BF16)
    y = _tail(x2, oa, ob, gates, wts, alpha)
    per_seq = lambda s: s.reshape((B, T) + s.shape[1:])
    return y.reshape(B, T, D), per_seq(k3), per_seq(v3), state, xc3[:, T - (CONV_W - 1):]


def _layer_weights(l, w_in, lambda_q1, lambda_k1, lambda_q2, lambda_k2, subln_w, conv_w, a_log,
                   dt_bias, gnorm_w, w_proj_a, w_proj_b, w_out, ln1_g, ln1_b, w_up, w_down,
                   ln2_g, ln2_b):
    nq = DA_HEADS * 2 * DA_HEAD_DIM
    nkv = DA_KV_HEADS * 2 * DA_HEAD_DIM
    nk = GD_K_HEADS * GD_DIM
    nv = GD_V_HEADS * GD_DIM
    b0 = nq
    b1 = b0 + 2 * nkv
    b2 = b1 + 2 * nk + nv
    b3 = b2 + nv
    b4 = b3 + 2 * GD_V_HEADS
    row = lambda p: p[l].reshape(1, -1)
    w_t = jnp.swapaxes(w_in, 1, 2)
    cols = lambda c0, n: _Weights(w_t, n, l, c0)
    tail = lambda c0, c1: _Weights(w_t[l, c0:c1], c1 - c0)
    return {
        "q": cols(0, nq), "k": cols(b0, nkv), "v": cols(b0 + nkv, nkv),
        "gq": cols(b1, nk), "gk": cols(b1 + nk, nk), "gv": cols(b1 + 2 * nk, nv),
        "gd": cols(b1, 2 * nk + nv), "z": cols(b2, nv),
        "ab": tail(b3, b4), "gates": tail(b4, w_in.shape[2]),
        "lams": (row(lambda_q1), row(lambda_k1), row(lambda_q2), row(lambda_k2)),
        "subln_w": row(subln_w), "conv_w": conv_w[l], "a_log": a_log[l], "dt_bias": dt_bias[l],
        "gnorm_w": row(gnorm_w),
        "w_proj_a": w_proj_a[l].astype(BF16), "w_proj_b": w_proj_b[l].astype(BF16),
        "w_out": w_out[l].astype(BF16), "ln1_g": row(ln1_g), "ln1_b": row(ln1_b),
        "w_up": w_up[l].astype(BF16), "w_down": w_down[l].astype(BF16),
        "ln2_g": row(ln2_g), "ln2_b": row(ln2_b),
    }


def kernel(x_prompt, x_sample, cache_k, cache_v, page_table, state_ssm, state_conv, w_in, lambda_q1, lambda_k1, lambda_q2, lambda_k2, subln_w, conv_w, a_log, dt_bias, gnorm_w, w_proj_a, w_proj_b, w_out, ln1_g, ln1_b, w_up, w_down, ln2_g, ln2_b):
    depth = w_in.shape[0]
    alpha = (2.0 * depth) ** 0.25
    bp, seq, d = x_prompt.shape
    n_pool, page = cache_k.shape[1], cache_k.shape[2]
    slab_view = (depth * n_pool, page * PAGE_SLABS, DA_HEAD_DIM)
    ck = cache_k.reshape(slab_view)
    cv = cache_v.reshape(depth, n_pool, page, DA_KV_HEADS, 2, DA_HEAD_DIM)
    cv = cv.transpose(0, 1, 2, 4, 3, 5).reshape(slab_view)
    yp, ys = x_prompt, x_sample
    outs = [[] for _ in range(8)]
    for l in range(depth):
        wts = _layer_weights(l, w_in, lambda_q1, lambda_k1, lambda_q2, lambda_k2, subln_w, conv_w,
                             a_log, dt_bias, gnorm_w, w_proj_a, w_proj_b, w_out, ln1_g, ln1_b,
                             w_up, w_down, ln2_g, ln2_b)
        lam_init = _lambda_init(l)
        per_seq = [_layer_prompt(yp[i], wts, lam_init, alpha) for i in range(bp)]
        yp, kp, vp, sp, cp = (_stack(t) for t in zip(*per_seq))
        ys, k_s, v_s, s_s, c_s = _layer_sample(
            ys, ck, cv, page_table + l * n_pool, state_ssm[l], state_conv[l], wts, lam_init, alpha)
        k_out = lambda s: s.reshape(s.shape[:2] + (DA_KV_HEADS, 2, DA_HEAD_DIM))
        v_out = lambda s: (s.reshape(s.shape[:2] + (2, DA_KV_HEADS, DA_HEAD_DIM))
                           .transpose(0, 1, 3, 2, 4).reshape(s.shape[:2] + (DA_KV_HEADS, -1)))
        for lst, val in zip(outs, (k_out(kp), v_out(vp), sp, cp, k_out(k_s), v_out(v_s), s_s, c_s)):
            lst.append(val)
    return (yp, ys) + tuple(_stack(o) for o in outs)
```

```python
import functools
import math

import jax
import jax.numpy as jnp
from jax import lax
from jax.experimental import pallas as pl
from jax.experimental.pallas import tpu as pltpu

F32 = jnp.float32
BF16 = jnp.bfloat16

DA_HEADS = 8
DA_KV_HEADS = 4
DA_HEAD_DIM = 128
DA_REP = DA_HEADS // DA_KV_HEADS
GD_K_HEADS = 16
GD_V_HEADS = 32
GD_DIM = 128
GD_REP = GD_V_HEADS // GD_K_HEADS
CONV_W = 4
CHUNK = 64
NORM_EPS = 1e-5
L2_EPS = 1e-6
NEG_INF = -1e30

V7X_VMEM_BYTES = 64 * 1024 * 1024
VMEM_LIMIT = V7X_VMEM_BYTES - 12 * 1024 * 1024

_NT = (((1,), (1,)), ((), ()))
_NN = (((1,), (0,)), ((), ()))


def _params(*sem):
    return pltpu.CompilerParams(dimension_semantics=sem, vmem_limit_bytes=VMEM_LIMIT)


def _dot(a, b, dn=_NN):
    return lax.dot_general(a, b, dn, preferred_element_type=F32)


def _split(a):
    hi = a.astype(BF16)
    lo = (a - hi.astype(F32)).astype(BF16)
    return hi, lo


def _dot3(a, b, dn=_NN):
    ah, al = _split(a)
    bh, bl = _split(b)
    return _dot(ah, bh, dn) + (_dot(al, bh, dn) + _dot(ah, bl, dn))


def _softplus(x):
    return jnp.maximum(x, 0.0) + jnp.log(1.0 + jnp.exp(-jnp.abs(x)))


def _sigmoid(x):
    return 1.0 / (1.0 + jnp.exp(-x))


def _lambda(lq1, lk1, lq2, lk2, lam_init):
    a = jnp.sum(lq1[...] * lk1[...], axis=-1, keepdims=True)
    b = jnp.sum(lq2[...] * lk2[...], axis=-1, keepdims=True)
    return jnp.exp(a) - jnp.exp(b) + lam_init


def _layer_norm(x, g, b):
    mu = jnp.mean(x, axis=-1, keepdims=True)
    xc = x - mu
    var = jnp.mean(xc * xc, axis=-1, keepdims=True)
    return xc * lax.rsqrt(var + NORM_EPS) * g + b


HALO = 8


class _Weights:
    def __init__(self, wt, ncols, layer=None, col0=0):
        self.w, self.ncols, self.layer, self.col0 = wt, ncols, layer, col0
        self.cast = wt.dtype != BF16

    def spec(self, K, tn):
        if self.w.ndim == 2:
            assert self.col0 == 0
            return pl.BlockSpec((tn, K), lambda j, i: (j, 0))
        off, layer = self.col0 // tn, self.layer
        assert self.col0 % tn == 0
        return pl.BlockSpec((None, tn, K), lambda j, i: (layer, j + off, 0),
                            pipeline_mode=pl.Buffered(1))

    def scratch(self, K, tn):
        return [pltpu.VMEM((tn, K), BF16)] if self.cast else []


def _project_tile(x_ref, w_ref, scr, cast):
    if not cast:
        return _dot(x_ref[...], w_ref[...], _NT)
    wb_sc = scr[0]

    @pl.when(pl.program_id(1) == 0)
    def _():
        wb_sc[...] = w_ref[...].astype(BF16)

    return _dot(x_ref[...], wb_sc[...], _NT)


def _proj_kernel(x_ref, w_ref, *rest, n_out, cast, scale, act):
    acc = _project_tile(x_ref, w_ref, rest[n_out:], cast)
    if scale != 1.0:
        acc = acc * scale
    if act == "silu":
        acc = acc * _sigmoid(acc)
    elif act == "sigmoid":
        acc = _sigmoid(acc)
    for o in rest[:n_out]:
        o[...] = acc.astype(o.dtype)


def _project(x, wt, out_dtypes, *, scale=1.0, act=None, tm=1024, tn=1024):
    M, K = x.shape
    N = wt.ncols
    tm = min(tm, M)
    tn = min(tn, N)
    return pl.pallas_call(
        functools.partial(_proj_kernel, n_out=len(out_dtypes), cast=wt.cast, scale=scale, act=act),
        grid=(N // tn, M // tm),
        in_specs=[pl.BlockSpec((tm, K), lambda j, i: (i, 0)), wt.spec(K, tn)],
        out_specs=[pl.BlockSpec((tm, tn), lambda j, i: (i, j)) for _ in out_dtypes],
        out_shape=[jax.ShapeDtypeStruct((M, N), d) for d in out_dtypes],
        scratch_shapes=wt.scratch(K, tn),
        compiler_params=_params("parallel", "arbitrary"),
        name="in_proj",
    )(x, wt.w)


def _proj_slab_kernel(x_ref, w_ref, o3_ref, ob_ref, *scr, cast, order):
    acc = _project_tile(x_ref, w_ref, scr, cast)
    ob_ref[...] = acc.astype(ob_ref.dtype)
    for slab, cb in enumerate(order):
        o3_ref[:, slab, :] = acc[:, cb * DA_HEAD_DIM:(cb + 1) * DA_HEAD_DIM]


def _project_slabs(x, wt, order, *, tm=1024):
    M, K = x.shape
    N = wt.ncols
    tm = min(tm, M)
    return pl.pallas_call(
        functools.partial(_proj_slab_kernel, cast=wt.cast, order=order),
        grid=(1, M // tm),
        in_specs=[pl.BlockSpec((tm, K), lambda j, i: (i, 0)), wt.spec(K, N)],
        out_specs=[pl.BlockSpec((tm, len(order), DA_HEAD_DIM), lambda j, i: (i, 0, 0)),
                   pl.BlockSpec((tm, N), lambda j, i: (i, 0))],
        out_shape=[jax.ShapeDtypeStruct((M, len(order), DA_HEAD_DIM), F32),
                   jax.ShapeDtypeStruct((M, N), BF16)],
        scratch_shapes=wt.scratch(K, N),
        compiler_params=_params("parallel", "arbitrary"),
        name="kv_proj",
    )(x, wt.w)


def _conv_silu_rows(xs_sc, cw_ref, o_ref, *, tm, tc, normalize, scale):
    y = cw_ref[CONV_W - 1:CONV_W, :] * xs_sc[HALO:HALO + tm, :]
    for back in range(1, CONV_W):
        y = y + cw_ref[CONV_W - 1 - back:CONV_W - back, :] * xs_sc[HALO - back:HALO - back + tm, :]
    y = y * _sigmoid(y)
    if normalize:
        for h in range(tc // GD_DIM):
            ys = y[:, h * GD_DIM:(h + 1) * GD_DIM]
            ss = jnp.sum(ys * ys, axis=-1, keepdims=True)
            o_ref[:, h * GD_DIM:(h + 1) * GD_DIM] = ys * (lax.rsqrt(ss + L2_EPS) * scale)
    else:
        o_ref[...] = y
    xs_sc[0:HALO, :] = xs_sc[tm:tm + HALO, :]


def _proj_conv_kernel(x_ref, w_ref, cw_ref, o_ref, tail_ref, *scr, tm, tn, cast, normalize, scale):
    xs_sc = scr[-1]

    @pl.when(pl.program_id(1) == 0)
    def _():
        xs_sc[0:HALO, :] = jnp.zeros((HALO, tn), F32)

    xs_sc[HALO:HALO + tm, :] = _project_tile(x_ref, w_ref, scr, cast)
    tail_ref[...] = xs_sc[tm:tm + HALO, :]
    _conv_silu_rows(xs_sc, cw_ref, o_ref, tm=tm, tc=tn, normalize=normalize, scale=scale)


def _project_conv(x, wt, conv_w, conv_col0, *, normalize, scale=1.0, tm=1024, tn=1024):
    M, K = x.shape
    N = wt.ncols
    tm = min(tm, M)
    tn = min(tn, N)
    coff = conv_col0 // tn
    assert conv_col0 % tn == 0
    return pl.pallas_call(
        functools.partial(_proj_conv_kernel, tm=tm, tn=tn, cast=wt.cast, normalize=normalize,
                          scale=scale),
        grid=(N // tn, M // tm),
        in_specs=[pl.BlockSpec((tm, K), lambda j, i: (i, 0)), wt.spec(K, tn),
                  pl.BlockSpec((CONV_W, tn), lambda j, i: (0, j + coff))],
        out_specs=[pl.BlockSpec((tm, tn), lambda j, i: (i, j)),
                   pl.BlockSpec((HALO, tn), lambda j, i: (0, j))],
        out_shape=[jax.ShapeDtypeStruct((M, N), F32), jax.ShapeDtypeStruct((HALO, N), F32)],
        scratch_shapes=wt.scratch(K, tn) + [pltpu.VMEM((tm + HALO, tn), F32)],
        compiler_params=_params("parallel", "arbitrary"),
        name="gd_proj_conv",
    )(x, wt.w, conv_w)


def _subln_out(acc0, l0, acc1, l1, lam, subw, lam_init):
    o = acc0 / l0 - lam * (acc1 / l1)
    o = o * lax.rsqrt(jnp.mean(o * o, axis=-1, keepdims=True) + NORM_EPS)
    return o * subw * (1.0 - lam_init)


def _attn_prompt_kernel(q_ref, k_ref, v_ref, subw_ref, lq1, lk1, lq2, lk2, o_ref,
                        m_sc, l_sc, acc_sc, *, tq, tk, lam_init):
    qi = pl.program_id(1)
    ki = pl.program_id(2)
    last = (qi * tq + tq - 1) // tk

    @pl.when(ki == 0)
    def _():
        m_sc[...] = jnp.full(m_sc.shape, -jnp.inf, F32)
        l_sc[...] = jnp.zeros(l_sc.shape, F32)
        acc_sc[...] = jnp.zeros(acc_sc.shape, F32)

    def step(masked):
        v = v_ref[...]
        if masked:
            ahead = (lax.broadcasted_iota(jnp.int32, (tq, tk), 1)
                     - lax.broadcasted_iota(jnp.int32, (tq, tk), 0))
            keep = ahead <= qi * tq - ki * tk
        for r in range(DA_REP):
            for c in range(2):
                idx = r * 2 + c
                q = q_ref[:, idx * DA_HEAD_DIM:(idx + 1) * DA_HEAD_DIM]
                k = k_ref[:, c * DA_HEAD_DIM:(c + 1) * DA_HEAD_DIM]
                s = _dot(q, k, _NT)
                if masked:
                    s = jnp.where(keep, s, NEG_INF)
                m_prev = m_sc[idx]
                m_new = jnp.maximum(m_prev, jnp.max(s, axis=-1, keepdims=True))
                alpha = jnp.exp2(m_prev - m_new)
                p = jnp.exp2(s - m_new)
                l_sc[idx] = alpha * l_sc[idx] + jnp.sum(p, axis=-1, keepdims=True)
                acc_sc[idx] = alpha * acc_sc[idx] + _dot(p.astype(BF16), v)
                m_sc[idx] = m_new

    @pl.when(ki < last)
    def _():
        step(False)

    @pl.when(ki == last)
    def _():
        step(True)
        lam = _lambda(lq1, lk1, lq2, lk2, lam_init)
        e = 2 * DA_HEAD_DIM
        for r in range(DA_REP):
            o = _subln_out(acc_sc[r * 2], l_sc[r * 2], acc_sc[r * 2 + 1], l_sc[r * 2 + 1],
                           lam, subw_ref[...], lam_init)
            o_ref[:, r * e:(r + 1) * e] = o.astype(o_ref.dtype)


def _attn_prompt(q, k, v, subw, lams, lam_init, *, tq=512, tk=2048):
    L = q.shape[0]
    tq = min(tq, L)
    tk = min(tk, L)
    assert tk % tq == 0
    e = 2 * DA_HEAD_DIM
    vec = pl.BlockSpec((1, DA_HEAD_DIM), lambda g, i, j: (0, 0))
    kblk = lambda i, j: jnp.minimum(j, (i * tq + tq - 1) // tk)
    return pl.pallas_call(
        functools.partial(_attn_prompt_kernel, tq=tq, tk=tk, lam_init=lam_init),
        grid=(DA_KV_HEADS, L // tq, L // tk),
        in_specs=[pl.BlockSpec((tq, DA_REP * e), lambda g, i, j: (i, g)),
                  pl.BlockSpec((tk, e), lambda g, i, j: (kblk(i, j), g)),
                  pl.BlockSpec((tk, e), lambda g, i, j: (kblk(i, j), g)),
                  pl.BlockSpec((1, e), lambda g, i, j: (0, 0)),
                  vec, vec, vec, vec],
        out_specs=pl.BlockSpec((tq, DA_REP * e), lambda g, i, j: (i, g)),
        out_shape=jax.ShapeDtypeStruct((L, DA_HEADS * e), BF16),
        scratch_shapes=[pltpu.VMEM((2 * DA_REP, tq, 1), F32),
                        pltpu.VMEM((2 * DA_REP, tq, 1), F32),
                        pltpu.VMEM((2 * DA_REP, tq, e), F32)],
        compiler_params=_params("parallel", "parallel", "arbitrary"),
        name="attn_prompt",
    )(q, k, v, subw, *lams)


Q_ROWS = 16


PAGE_SLABS = 2 * DA_KV_HEADS
PAGES_PER_STEP = 16


def _attn_sample_kernel(pt_ref, q_ref, *rest, pages, page, dec_seq, lam_init):
    k_refs = rest[:pages]
    v_refs = rest[pages:2 * pages]
    kn_ref, vn_ref, subw_ref, lq1, lk1, lq2, lk2, o_ref, m_sc, l_sc, acc_sc = rest[2 * pages:]
    j = pl.program_id(1)
    rows = 2 * DA_KV_HEADS * Q_ROWS

    @pl.when(j == 0)
    def _():
        m_sc[...] = jnp.full(m_sc.shape, -jnp.inf, F32)
        l_sc[...] = jnp.zeros(l_sc.shape, F32)
        acc_sc[...] = jnp.zeros(acc_sc.shape, F32)

    def slab(ref, i):
        return ref[0, pl.ds(i, page, stride=PAGE_SLABS), :]

    def update(k_list, v_list, keep):
        s = []
        for idx in range(2 * DA_KV_HEADS):
            k = jnp.concatenate([slab(kr, idx) for kr in k_list], axis=0).astype(BF16)
            s.append(_dot(q_ref[0, idx], k, _NT))
        s = jnp.concatenate(s, axis=0)
        if keep is not None:
            s = jnp.where(keep, s, NEG_INF)
        m_prev = m_sc[...]
        m_new = jnp.maximum(m_prev, jnp.max(s, axis=-1, keepdims=True))
        alpha = jnp.exp2(m_prev - m_new)
        p = jnp.exp2(s - m_new)
        l_sc[...] = alpha * l_sc[...] + jnp.sum(p, axis=-1, keepdims=True)
        m_sc[...] = m_new
        p = p.astype(BF16)
        pv = []
        for g in range(DA_KV_HEADS):
            v = jnp.concatenate(
                [jnp.concatenate([slab(vr, g), slab(vr, DA_KV_HEADS + g)], axis=1) for vr in v_list],
                axis=0).astype(BF16)
            pv.append(_dot(p[g * 2 * Q_ROWS:(g + 1) * 2 * Q_ROWS], v))
        acc_sc[...] = alpha * acc_sc[...] + jnp.concatenate(pv, axis=0)

    update(k_refs, v_refs, None)

    @pl.when(j == pl.num_programs(1) - 1)
    def _():
        row = lax.broadcasted_iota(jnp.int32, (rows, page), 0)
        col = lax.broadcasted_iota(jnp.int32, (rows, page), 1)
        update([kn_ref], [vn_ref], col <= (row % Q_ROWS) % dec_seq)
        lam = _lambda(lq1, lk1, lq2, lk2, lam_init)
        for g in range(DA_KV_HEADS):
            r0 = slice(2 * g * Q_ROWS, (2 * g + 1) * Q_ROWS)
            r1 = slice((2 * g + 1) * Q_ROWS, (2 * g + 2) * Q_ROWS)
            o = _subln_out(acc_sc[r0, :], l_sc[r0, :], acc_sc[r1, :], l_sc[r1, :],
                           lam, subw_ref[...], lam_init)
            o_ref[0, g] = o.astype(o_ref.dtype)


def _attn_sample(q, cache_k, cache_v, page_table, k_new, v_new, subw, lams, lam_init, dec_seq):
    B = q.shape[0]
    n_pages = page_table.shape[1]
    page = cache_k.shape[1] // PAGE_SLABS
    pages = math.gcd(n_pages, PAGES_PER_STEP)
    e = 2 * DA_HEAD_DIM
    rows = 2 * DA_KV_HEADS * Q_ROWS

    def page_spec(i):
        return pl.BlockSpec((1, page * PAGE_SLABS, DA_HEAD_DIM),
                            lambda b, j, pt: (pt[b, j * pages + i], 0, 0))

    vec = pl.BlockSpec((1, DA_HEAD_DIM), lambda b, j, pt: (0, 0))
    new_spec = pl.BlockSpec((1, page * PAGE_SLABS, DA_HEAD_DIM), lambda b, j, pt: (b, 0, 0))
    grid_spec = pltpu.PrefetchScalarGridSpec(
        num_scalar_prefetch=1,
        grid=(B, n_pages // pages),
        in_specs=([pl.BlockSpec((1, 2 * DA_KV_HEADS, Q_ROWS, DA_HEAD_DIM), lambda b, j, pt: (b, 0, 0, 0))]
                  + [page_spec(i) for i in range(pages)] + [page_spec(i) for i in range(pages)]
                  + [new_spec, new_spec, pl.BlockSpec((1, e), lambda b, j, pt: (0, 0)),
                     vec, vec, vec, vec]),
        out_specs=pl.BlockSpec((1, DA_KV_HEADS, Q_ROWS, e), lambda b, j, pt: (b, 0, 0, 0)),
        scratch_shapes=[pltpu.VMEM((rows, 1), F32),
                        pltpu.VMEM((rows, 1), F32),
                        pltpu.VMEM((rows, e), F32)])
    return pl.pallas_call(
        functools.partial(_attn_sample_kernel, pages=pages, page=page, dec_seq=dec_seq,
                          lam_init=lam_init),
        grid_spec=grid_spec,
        out_shape=jax.ShapeDtypeStruct((B, DA_KV_HEADS, Q_ROWS, e), F32),
        compiler_params=_params("parallel", "arbitrary"),
        name="attn_sample",
    )(page_table, q, *([cache_k] * pages), *([cache_v] * pages), k_new, v_new, subw, *lams)


def _conv_kernel(x_ref, w_ref, o_ref, xs_sc, *, tm, tc, normalize, scale):
    @pl.when(pl.program_id(1) == 0)
    def _():
        xs_sc[0:HALO, :] = jnp.zeros((HALO, tc), F32)

    xs_sc[HALO:HALO + tm, :] = x_ref[...]
    _conv_silu_rows(xs_sc, w_ref, o_ref, tm=tm, tc=tc, normalize=normalize, scale=scale)


def _conv_silu(x, w, col0, ncols, *, normalize, scale=1.0, tm=512, tc=512):
    L = x.shape[0]
    tm = min(tm, L)
    off = col0 // tc
    return pl.pallas_call(
        functools.partial(_conv_kernel, tm=tm, tc=tc, normalize=normalize, scale=scale),
        grid=(ncols // tc, L // tm),
        in_specs=[pl.BlockSpec((tm, tc), lambda c, i: (i, c + off)),
                  pl.BlockSpec((CONV_W, tc), lambda c, i: (0, c + off))],
        out_specs=pl.BlockSpec((tm, tc), lambda c, i: (i, c)),
        out_shape=jax.ShapeDtypeStruct((L, ncols), F32),
        scratch_shapes=[pltpu.VMEM((tm + HALO, tc), F32)],
        compiler_params=_params("parallel", "arbitrary"),
        name="conv_silu",
    )(x, w)


GROUP_CHUNKS = 4


def _gdn_chunk_kernel(q_ref, k_ref, v_ref, arow_ref, brow_ref, alog_ref, dtb_ref, zs_ref, gw_ref,
                      o_ref, s_ref, s_sc, *, groups, C, cpg):
    gt = cpg * C

    @pl.when(pl.program_id(1) == 0)
    def _():
        s_sc[...] = jnp.zeros(s_sc.shape, F32)
    shift = C.bit_length() - 1
    row = lax.broadcasted_iota(jnp.int32, (gt, gt), 0)
    col = lax.broadcasted_iota(jnp.int32, (gt, gt), 1)
    same = (row >> shift) == (col >> shift)
    tril = same & (col <= row)
    strict = same & (col < row)
    diag = row == col
    eye = jnp.where(diag, 1.0, 0.0)
    r2 = lax.broadcasted_iota(jnp.int32, (cpg * GD_DIM, gt), 0)
    c2 = lax.broadcasted_iota(jnp.int32, (cpg * GD_DIM, gt), 1)
    ke_mask = (r2 >> (GD_DIM.bit_length() - 1)) == (c2 >> shift)
    units = []
    for gi in range(groups):
        rows = slice(gi * gt, (gi + 1) * gt)
        q = q_ref[rows, :]
        k = k_ref[rows, :]
        k16 = k.astype(BF16)
        raw = _dot(jnp.concatenate([q.astype(BF16), k16], axis=0), k16, _NT)
        for j in range(GD_REP):
            neg_a = -jnp.exp(alog_ref[j])
            dtb = dtb_ref[j]
            g_row = neg_a * _softplus(arow_ref[j, gi] + dtb)
            beta = jnp.sum(jnp.where(diag, _sigmoid(brow_ref[j, gi]), 0.0), axis=1, keepdims=True)
            gc_col = jnp.sum(jnp.where(tril, g_row, 0.0), axis=1, keepdims=True)
            gc_row = jnp.sum(jnp.where(diag, gc_col, 0.0), axis=0, keepdims=True)
            gl_col = jnp.sum(jnp.where(same, g_row, 0.0), axis=1, keepdims=True)
            decay = jnp.where(tril, jnp.exp(jnp.where(tril, gc_col - gc_row, 0.0)), 0.0)
            lm = jnp.where(strict, beta * raw[gt:] * decay, 0.0)
            units.append(dict(gi=gi, j=j, rows=rows, beta=beta, gc_col=gc_col, gl_col=gl_col,
                              qk=raw[:gt] * decay, x=eye - lm, p16=lm.astype(BF16)))
    n = 1
    while 2 * n < C:
        for u in units:
            u["p16"] = _dot(u["p16"], u["p16"]).astype(BF16)
        for u in units:
            u["x"] = u["x"] + _dot(u["x"].astype(BF16), u["p16"])
        n *= 2
    for u in units:
        k = k_ref[u["rows"], :]
        v = v_ref[u["rows"], u["j"] * GD_DIM:(u["j"] + 1) * GD_DIM]
        u["eg"] = jnp.exp(u["gc_col"])
        u["wu"] = _dot(u["x"].astype(BF16),
                       jnp.concatenate([k * (u["beta"] * u["eg"]), v * u["beta"]], axis=1).astype(BF16))
    for u in units:
        q = q_ref[u["rows"], :]
        k = k_ref[u["rows"], :]
        ket = (k * jnp.exp(u["gl_col"] - u["gc_col"])).T
        ke_bd = jnp.where(ke_mask, jnp.concatenate([ket] * cpg, axis=0), 0.0)
        u["res"] = _dot(jnp.concatenate([ke_bd, u["qk"]], axis=0).astype(BF16), u["wu"].astype(BF16))
        u["qe"] = q * u["eg"]
    s = [s_sc[j] for j in range(GD_REP)]
    for u in units:
        gi, j, res = u["gi"], u["j"], u["res"]
        cols = slice(j * GD_DIM, (j + 1) * GD_DIM)
        for c in range(cpg):
            top = slice(c * GD_DIM, (c + 1) * GD_DIM)
            bot = slice(cpg * GD_DIM + c * C, cpg * GD_DIM + (c + 1) * C)
            rows = slice(gi * gt + c * C, gi * gt + (c + 1) * C)
            mq = jnp.concatenate([res[top, :GD_DIM], u["qe"][c * C:(c + 1) * C] - res[bot, :GD_DIM]],
                                 axis=0).astype(BF16)
            r = _dot(mq, s[j].astype(BF16))
            s[j] = jnp.exp(u["gl_col"][c * C:c * C + 1, :]) * s[j] - r[:GD_DIM] + res[top, GD_DIM:]
            o = r[GD_DIM:] + res[bot, GD_DIM:]
            o = o * lax.rsqrt(jnp.mean(o * o, axis=-1, keepdims=True) + NORM_EPS) * gw_ref[...]
            o_ref[rows, cols] = (o * zs_ref[rows, cols].astype(F32)).astype(o_ref.dtype)
    for j in range(GD_REP):
        s_sc[j] = s[j]

    @pl.when(pl.program_id(1) == pl.num_programs(1) - 1)
    def _():
        s_ref[...] = s_sc[...]


def _gdn_prompt(qn, kn, vv, zs, a, b, a_log, dt_bias, gnorm_w, *, groups=4):
    L = qn.shape[0]
    H = GD_V_HEADS
    C = min(CHUNK, L)
    assert C & (C - 1) == 0 and L % C == 0
    N = L // C
    cpg = math.gcd(GROUP_CHUNKS, N)
    gt = cpg * C
    ng = N // cpg
    groups = math.gcd(groups, ng)
    a_t = a.T.reshape(H, ng, gt)
    b_t = b.T.reshape(H, ng, gt)
    hvec = lambda x: x.reshape(H, 1, 1)
    grp = lambda *shape: pl.BlockSpec((GD_REP, groups) + shape, lambda hk, n: (hk, n, 0, 0))
    tokens = lambda width: pl.BlockSpec((groups * gt, width), lambda hk, n: (n, hk))
    out, state = pl.pallas_call(
        functools.partial(_gdn_chunk_kernel, groups=groups, C=C, cpg=cpg),
        grid=(GD_K_HEADS, ng // groups),
        in_specs=[tokens(GD_DIM), tokens(GD_DIM), tokens(GD_REP * GD_DIM),
                  grp(1, gt), grp(1, gt),
                  pl.BlockSpec((GD_REP, 1, 1), lambda hk, n: (hk, 0, 0)),
                  pl.BlockSpec((GD_REP, 1, 1), lambda hk, n: (hk, 0, 0)),
                  tokens(GD_REP * GD_DIM),
                  pl.BlockSpec((1, GD_DIM), lambda hk, n: (0, 0))],
        out_specs=[tokens(GD_REP * GD_DIM),
                   pl.BlockSpec((GD_REP, GD_DIM, GD_DIM), lambda hk, n: (hk, 0, 0))],
        out_shape=[jax.ShapeDtypeStruct((L, H * GD_DIM), BF16),
                   jax.ShapeDtypeStruct((H, GD_DIM, GD_DIM), F32)],
        scratch_shapes=[pltpu.VMEM((GD_REP, GD_DIM, GD_DIM), F32)],
        compiler_params=_params("parallel", "arbitrary"),
        name="gdn_chunk",
    )(qn, kn, vv, a_t[:, :, None, :], b_t[:, :, None, :], hvec(a_log), hvec(dt_bias), zs, gnorm_w)
    return out, state


def _gdn_sample_kernel(qt_ref, kt_ref, v_ref, zs_ref, a_ref, b_ref, alog_ref, dtb_ref, gw_ref,
                       s_ref, o_ref, so_ref, *, T):
    def key_head(hk, carry):
        qt = qt_ref[0, hk]
        kt = kt_ref[0, hk]
        heads = [hk * GD_REP + j for j in range(GD_REP)]
        s = [s_ref[0, h] for h in heads]
        g = [-jnp.exp(alog_ref[h]) * _softplus(a_ref[0, h] + dtb_ref[h]) for h in heads]
        beta = [_sigmoid(b_ref[0, h]) for h in heads]
        v = [v_ref[0, h] for h in heads]
        outs = [[] for _ in heads]
        for t in range(T):
            kc = kt[:, t:t + 1]
            qc = qt[:, t:t + 1]
            for j in range(GD_REP):
                sj = s[j] * jnp.exp(g[j][t:t + 1, :])
                v_new = beta[j][t:t + 1, :] * (v[j][t:t + 1, :] - jnp.sum(kc * sj, axis=0, keepdims=True))
                s[j] = sj + kc * v_new
                outs[j].append(jnp.sum(qc * s[j], axis=0, keepdims=True))
        for j, h in enumerate(heads):
            so_ref[0, h] = s[j]
            o = jnp.concatenate(outs[j], axis=0)
            o = o * lax.rsqrt(jnp.mean(o * o, axis=-1, keepdims=True) + NORM_EPS) * gw_ref[...]
            o_ref[0, h] = o * zs_ref[0, h]
        return carry

    lax.fori_loop(0, GD_K_HEADS, key_head, 0)


def _gdn_sample(qt, kt, v, z, a, b, a_log, dt_bias, gnorm_w, state):
    B, H, T, _ = v.shape
    per_b = lambda *shape: pl.BlockSpec((1,) + shape, lambda i: (i,) + (0,) * len(shape))
    hpar = pl.BlockSpec((H, 1, GD_DIM), lambda i: (0, 0, 0))
    return pl.pallas_call(
        functools.partial(_gdn_sample_kernel, T=T),
        grid=(B,),
        in_specs=[per_b(GD_K_HEADS, GD_DIM, T), per_b(GD_K_HEADS, GD_DIM, T),
                  per_b(H, T, GD_DIM), per_b(H, T, GD_DIM), per_b(H, T, GD_DIM), per_b(H, T, GD_DIM),
                  hpar, hpar, pl.BlockSpec((1, GD_DIM), lambda i: (0, 0)),
                  per_b(H, GD_DIM, GD_DIM)],
        out_specs=[per_b(H, T, GD_DIM), per_b(H, GD_DIM, GD_DIM)],
        out_shape=[jax.ShapeDtypeStruct((B, H, T, GD_DIM), F32),
                   jax.ShapeDtypeStruct((B, H, GD_DIM, GD_DIM), F32)],
        compiler_params=_params("parallel"),
        name="gdn_sample",
    )(qt, kt, v, z, a, b, a_log, dt_bias, gnorm_w, state)


def _merge_kernel(oa_ref, ob_ref, wa_ref, wb_ref, ga_ref, gb_ref, o_ref):
    ya = _dot(oa_ref[...], wa_ref[...])
    yb = _dot(ob_ref[...], wb_ref[...])
    o_ref[...] = (ga_ref[...].astype(F32) * ya + gb_ref[...].astype(F32) * yb).astype(o_ref.dtype)


def _merge(oa, ob, wa, wb, gates, *, tm=512, tn=512):
    M = oa.shape[0]
    D = wa.shape[1]
    tm = min(tm, M)
    tn = min(tn, D)
    nj = D // tn
    return pl.pallas_call(
        _merge_kernel,
        grid=(nj, M // tm),
        in_specs=[pl.BlockSpec((tm, oa.shape[1]), lambda j, i: (i, 0)),
                  pl.BlockSpec((tm, ob.shape[1]), lambda j, i: (i, 0)),
                  pl.BlockSpec((wa.shape[0], tn), lambda j, i: (0, j)),
                  pl.BlockSpec((wb.shape[0], tn), lambda j, i: (0, j)),
                  pl.BlockSpec((tm, tn), lambda j, i: (i, j)),
                  pl.BlockSpec((tm, tn), lambda j, i: (i, j + nj))],
        out_specs=pl.BlockSpec((tm, tn), lambda j, i: (i, j)),
        out_shape=jax.ShapeDtypeStruct((M, D), BF16),
        compiler_params=_params("parallel", "parallel"),
        name="merge",
    )(oa, ob, wa, wb, gates, gates)


def _outproj_kernel(m_ref, w_ref, x_ref, g_ref, b_ref, o_ref, *, alpha):
    mix = _dot(m_ref[...], w_ref[...])
    o_ref[...] = _layer_norm(alpha * x_ref[...] + mix, g_ref[...], b_ref[...])


def _outproj_ln(merged, w, x, g, b, alpha, *, tm=256):
    M, D = x.shape
    tm = min(tm, M)
    row = pl.BlockSpec((1, D), lambda i: (0, 0))
    return pl.pallas_call(
        functools.partial(_outproj_kernel, alpha=alpha),
        grid=(M // tm,),
        in_specs=[pl.BlockSpec((tm, D), lambda i: (i, 0)),
                  pl.BlockSpec((D, D), lambda i: (0, 0)),
                  pl.BlockSpec((tm, D), lambda i: (i, 0)), row, row],
        out_specs=pl.BlockSpec((tm, D), lambda i: (i, 0)),
        out_shape=jax.ShapeDtypeStruct((M, D), F32),
        compiler_params=_params("parallel"),
        name="outproj_ln",
    )(merged, w, x, g, b)


def _ffn_kernel(h_ref, wu_ref, wd_ref, g_ref, b_ref, o_ref, hb_sc, acc_sc, *, alpha):
    f = pl.program_id(1)

    @pl.when(f == 0)
    def _():
        hb_sc[...] = h_ref[...].astype(BF16)
        acc_sc[...] = jnp.zeros(acc_sc.shape, F32)

    u = jnp.maximum(_dot(hb_sc[...], wu_ref[...]), 0.0)
    acc_sc[...] += _dot((u * u).astype(BF16), wd_ref[...])

    @pl.when(f == pl.num_programs(1) - 1)
    def _():
        o_ref[...] = _layer_norm(alpha * h_ref[...] + acc_sc[...], g_ref[...], b_ref[...])


def _ffn_ln(h, wu, wd, g, b, alpha, *, tm=512, tf=512):
    M, D = h.shape
    F = wu.shape[1]
    tm = min(tm, M)
    tf = min(tf, F)
    row = pl.BlockSpec((1, D), lambda i, f: (0, 0))
    return pl.pallas_call(
        functools.partial(_ffn_kernel, alpha=alpha),
        grid=(M // tm, F // tf),
        in_specs=[pl.BlockSpec((tm, D), lambda i, f: (i, 0)),
                  pl.BlockSpec((D, tf), lambda i, f: (0, f)),
                  pl.BlockSpec((tf, D), lambda i, f: (f, 0)), row, row],
        out_specs=pl.BlockSpec((tm, D), lambda i, f: (i, 0)),
        out_shape=jax.ShapeDtypeStruct((M, D), F32),
        scratch_shapes=[pltpu.VMEM((tm, D), BF16), pltpu.VMEM((tm, D), F32)],
        compiler_params=_params("parallel", "arbitrary"),
        name="ffn_ln",
    )(h, wu, wd, g, b)


def _stack(parts):
    return parts[0][None] if len(parts) == 1 else jnp.stack(parts)


def _lambda_init(layer):
    return 0.8 - 0.6 * math.exp(-0.3 * layer)


K_SLAB_ORDER = tuple(range(PAGE_SLABS))
V_SLAB_ORDER = tuple(g * 2 + half for half in range(2) for g in range(DA_KV_HEADS))


def _shared_projections(xb, wts):
    (q,) = _project(xb, wts["q"], [BF16], scale=DA_HEAD_DIM ** -0.5 * math.log2(math.e))
    k3, kb = _project_slabs(xb, wts["k"], K_SLAB_ORDER)
    v3, vb = _project_slabs(xb, wts["v"], V_SLAB_ORDER)
    (zs,) = _project(xb, wts["z"], [BF16], act="silu")
    (ab,) = _project(xb, wts["ab"], [F32])
    (gates,) = _project(xb, wts["gates"], [BF16], act="sigmoid")
    return q, k3, kb, v3, vb, zs, ab, gates


def _tail(x, oa, ob, gates, wts, alpha):
    merged = _merge(oa, ob, wts["w_proj_a"], wts["w_proj_b"], gates)
    h = _outproj_ln(merged, wts["w_out"], x, wts["ln1_g"], wts["ln1_b"], alpha)
    return _ffn_ln(h, wts["w_up"], wts["w_down"], wts["ln2_g"], wts["ln2_b"], alpha)


def _layer_prompt(x, wts, lam_init, alpha):
    nk = GD_K_HEADS * GD_DIM
    xb = x.astype(BF16)
    q, k3, kb, v3, vb, zs, ab, gates = _shared_projections(xb, wts)
    oa = _attn_prompt(q, kb, vb, wts["subln_w"], wts["lams"], lam_init)
    qn, tq = _project_conv(xb, wts["gq"], wts["conv_w"], 0, normalize=True, scale=GD_DIM ** -0.5)
    kn, tk = _project_conv(xb, wts["gk"], wts["conv_w"], nk, normalize=True)
    vv, tv = _project_conv(xb, wts["gv"], wts["conv_w"], 2 * nk, normalize=False)
    ob, state = _gdn_prompt(qn, kn, vv, zs, ab[:, :GD_V_HEADS], ab[:, GD_V_HEADS:],
                            wts["a_log"], wts["dt_bias"], wts["gnorm_w"])
    y = _tail(x, oa, ob, gates, wts, alpha)
    conv_rows = jnp.concatenate([tq, tk, tv], axis=1)[HALO - (CONV_W - 1):]
    return y, k3, v3, state, conv_rows


def _layer_sample(x, cache_k, cache_v, page_table, state_ssm, state_conv, wts, lam_init, alpha):
    B, T, D = x.shape
    M = B * T
    H = GD_V_HEADS
    nk = GD_K_HEADS * GD_DIM
    slabs = cache_k.shape[1]
    x2 = x.reshape(M, D)
    xb = x2.astype(BF16)
    q, k3, _, v3, _, zs, ab, gates = _shared_projections(xb, wts)
    (xc,) = _project(xb, wts["gd"], [F32])
    qh = q.reshape(B, T, DA_KV_HEADS, DA_REP, 2, DA_HEAD_DIM).transpose(0, 2, 4, 3, 1, 5)
    qh = qh.reshape(B, 2 * DA_KV_HEADS, DA_REP * T, DA_HEAD_DIM)
    qh = jnp.pad(qh, ((0, 0), (0, 0), (0, Q_ROWS - DA_REP * T), (0, 0)))
    as_page = lambda s: jnp.pad(s.reshape(B, T * PAGE_SLABS, DA_HEAD_DIM),
                                ((0, 0), (0, slabs - T * PAGE_SLABS), (0, 0)))
    k_new, v_new = as_page(k3), as_page(v3)
    oa = _attn_sample(qh, cache_k, cache_v, page_table, k_new, v_new, wts["subln_w"], wts["lams"],
                      lam_init, T)
    oa = oa[:, :, :DA_REP * T].reshape(B, DA_KV_HEADS, DA_REP, T, 2 * DA_HEAD_DIM)
    oa = oa.transpose(0, 3, 1, 2, 4).reshape(M, DA_HEADS * 2 * DA_HEAD_DIM).astype(BF16)
    cdim = xc.shape[1]
    xc3 = xc.reshape(B, T, cdim)
    grp = jnp.concatenate([jnp.zeros((B, HALO - (CONV_W - 1) - T, cdim), F32), state_conv, xc3], axis=1)
    grp = grp.reshape(B * HALO, cdim)
    take = lambda y: y.reshape(B, HALO, -1)[:, HALO - T:]
    qn = take(_conv_silu(grp, wts["conv_w"], 0, nk, normalize=True, scale=GD_DIM ** -0.5))
    kn = take(_conv_silu(grp, wts["conv_w"], nk, nk, normalize=True))
    vv = take(_conv_silu(grp, wts["conv_w"], 2 * nk, H * GD_DIM, normalize=False))
    to_cols = lambda y: y.reshape(B, T, GD_K_HEADS, GD_DIM).transpose(0, 2, 3, 1)
    to_heads = lambda y: y.reshape(B, T, H, GD_DIM).transpose(0, 2, 1, 3)
    lanes = lambda y: jnp.broadcast_to(y.reshape(B, T, H).transpose(0, 2, 1)[..., None], (B, H, T, GD_DIM))
    hpar = lambda p: jnp.broadcast_to(p.reshape(H, 1, 1), (H, 1, GD_DIM))
    ob, state = _gdn_sample(to_cols(qn), to_cols(kn), to_heads(vv), to_heads(zs.astype(F32)),
                            lanes(ab[:, :H]), lanes(ab[:, H:]), hpar(wts["a_log"]),
                            hpar(wts["dt_bias"]), wts["gnorm_w"], state_ssm)
    ob = ob.transpose(0, 2, 1, 3).reshape(M, H * GD_DIM).astype(BF16)
    y = _tail(x2, oa, ob, gates, wts, alpha)
    per_seq = lambda s: s.reshape((B, T) + s.shape[1:])
    return y.reshape(B, T, D), per_seq(k3), per_seq(v3), state, xc3[:, T - (CONV_W - 1):]


def _layer_weights(l, w_in, lambda_q1, lambda_k1, lambda_q2, lambda_k2, subln_w, conv_w, a_log,
                   dt_bias, gnorm_w, w_proj_a, w_proj_b, w_out, ln1_g, ln1_b, w_up, w_down,
                   ln2_g, ln2_b):
    nq = DA_HEADS * 2 * DA_HEAD_DIM
    nkv = DA_KV_HEADS * 2 * DA_HEAD_DIM
    nk = GD_K_HEADS * GD_DIM
    nv = GD_V_HEADS * GD_DIM
    b0 = nq
    b1 = b0 + 2 * nkv
    b2 = b1 + 2 * nk + nv
    b3 = b2 + nv
    b4 = b3 + 2 * GD_V_HEADS
    row = lambda p: p[l].reshape(1, -1)
    w_t = jnp.swapaxes(w_in, 1, 2)
    cols = lambda c0, n: _Weights(w_t, n, l, c0)
    tail = lambda c0, c1: _Weights(w_t[l, c0:c1], c1 - c0)
    return {
        "q": cols(0, nq), "k": cols(b0, nkv), "v": cols(b0 + nkv, nkv),
        "gq": cols(b1, nk), "gk": cols(b1 + nk, nk), "gv": cols(b1 + 2 * nk, nv),
        "gd": cols(b1, 2 * nk + nv), "z": cols(b2, nv),
        "ab": tail(b3, b4), "gates": tail(b4, w_in.shape[2]),
        "lams": (row(lambda_q1), row(lambda_k1), row(lambda_q2), row(lambda_k2)),
        "subln_w": row(subln_w), "conv_w": conv_w[l], "a_log": a_log[l], "dt_bias": dt_bias[l],
        "gnorm_w": row(gnorm_w),
        "w_proj_a": w_proj_a[l].astype(BF16), "w_proj_b": w_proj_b[l].astype(BF16),
        "w_out": w_out[l].astype(BF16), "ln1_g": row(ln1_g), "ln1_b": row(ln1_b),
        "w_up": w_up[l].astype(BF16), "w_down": w_down[l].astype(BF16),
        "ln2_g": row(ln2_g), "ln2_b": row(ln2_b),
    }


def kernel(x_prompt, x_sample, cache_k, cache_v, page_table, state_ssm, state_conv, w_in, lambda_q1, lambda_k1, lambda_q2, lambda_k2, subln_w, conv_w, a_log, dt_bias, gnorm_w, w_proj_a, w_proj_b, w_out, ln1_g, ln1_b, w_up, w_down, ln2_g, ln2_b):
    depth = w_in.shape[0]
    alpha = (2.0 * depth) ** 0.25
    bp, seq, d = x_prompt.shape
    n_pool, page = cache_k.shape[1], cache_k.shape[2]
    slab_view = (depth * n_pool, page * PAGE_SLABS, DA_HEAD_DIM)
    ck = cache_k.reshape(slab_view)
    cv = cache_v.reshape(depth, n_pool, page, DA_KV_HEADS, 2, DA_HEAD_DIM)
    cv = cv.transpose(0, 1, 2, 4, 3, 5).reshape(slab_view)
    yp, ys = x_prompt, x_sample
    outs = [[] for _ in range(8)]
    for l in range(depth):
        wts = _layer_weights(l, w_in, lambda_q1, lambda_k1, lambda_q2, lambda_k2, subln_w, conv_w,
                             a_log, dt_bias, gnorm_w, w_proj_a, w_proj_b, w_out, ln1_g, ln1_b,
                             w_up, w_down, ln2_g, ln2_b)
        lam_init = _lambda_init(l)
        per_seq = [_layer_prompt(yp[i], wts, lam_init, alpha) for i in range(bp)]
        yp, kp, vp, sp, cp = (_stack(t) for t in zip(*per_seq))
        ys, k_s, v_s, s_s, c_s = _layer_sample(
            ys, ck, cv, page_table + l * n_pool, state_ssm[l], state_conv[l], wts, lam_init, alpha)
        k_out = lambda s: s.reshape(s.shape[:2] + (DA_KV_HEADS, 2, DA_HEAD_DIM))
        v_out = lambda s: (s.reshape(s.shape[:2] + (2, DA_KV_HEADS, DA_HEAD_DIM))
                           .transpose(0, 1, 3, 2, 4).reshape(s.shape[:2] + (DA_KV_HEADS, -1)))
        for lst, val in zip(outs, (k_out(kp), v_out(vp), sp, cp, k_out(k_s), v_out(v_s), s_s, c_s)):
            lst.append(val)
    return (yp, ys) + tuple(_stack(o) for o in outs)
```
